```python
import jax
import jax.numpy as jnp
from jax import lax
import numpy as np


D_MODEL = 1024
BATCH = 4
SEQ = 4096
DEPTH = 2

N_META = 16
BLOCK = 128
EPS = 1e-6
FOX_HEADS = 8
FOX_HEAD_DIM = 64
MLA_HEADS = 8
MLA_Q_RANK = 256
MLA_KV_RANK = 128
MLA_NOPE_DIM = 64
MLA_ROPE_DIM = 32
MLA_V_DIM = 64
ROPE_THETA = 10000.0
SWA_Q_HEADS = 8
SWA_KV_HEADS = 2
SWA_HEAD_DIM = 64
WINDOW = 128
N_BRANCH = 3
BRANCH_WIDTH = FOX_HEADS * FOX_HEAD_DIM
D_FF = 2816
CONV_WIDTH = 3
IN_SPLITS = (FOX_HEADS * FOX_HEAD_DIM, FOX_HEADS * FOX_HEAD_DIM, FOX_HEADS * FOX_HEAD_DIM, FOX_HEADS,
             MLA_Q_RANK, MLA_KV_RANK, MLA_ROPE_DIM,
             SWA_Q_HEADS * SWA_HEAD_DIM, SWA_KV_HEADS * SWA_HEAD_DIM, SWA_KV_HEADS * SWA_HEAD_DIM,
             N_BRANCH * D_MODEL)
IN_WIDTH = sum(IN_SPLITS)

kernel_name = 'hybrid_fox_mla_swa_convffn'


def rms_norm(x, g):
    xf = x.astype(jnp.float32)
    y = xf * lax.rsqrt(jnp.mean(xf * xf, axis=-1, keepdims=True) + EPS)
    return (y * g.astype(jnp.float32)).astype(x.dtype)


def rope(x, pos):
    half = x.shape[-1] // 2
    freqs = ROPE_THETA ** (-jnp.arange(half, dtype=jnp.float32) / half)
    ang = pos.astype(jnp.float32)[:, None] * freqs[None, :]
    cos = jnp.cos(ang)[:, None, :]
    sin = jnp.sin(ang)[:, None, :]
    xf = x.astype(jnp.float32)
    x1, x2 = xf[..., :half], xf[..., half:]
    return jnp.concatenate([x1 * cos - x2 * sin, x2 * cos + x1 * sin], axis=-1).astype(x.dtype)


def alibi_slopes(n_heads):
    return jnp.exp2(-8.0 * jnp.arange(1, n_heads + 1, dtype=jnp.float32) / n_heads)


def causal_block_attention(q, k, v, log_decay=None):
    B, L, H, dk = q.shape
    M = N_META
    nb = (L - M) // BLOCK
    scale = dk ** -0.5
    pos = jnp.arange(L)

    def attend(qb, qpos, kk, vv, kpos, qdec=None, kdec=None):
        s = jnp.einsum('bqhd,bkhd->bhqk', qb, kk, preferred_element_type=jnp.float32) * scale
        if qdec is not None:
            s = s + jnp.swapaxes(qdec, 1, 2)[..., :, None] - jnp.swapaxes(kdec, 1, 2)[..., None, :]
        s = jnp.where(kpos[None, :] <= qpos[:, None], s, -jnp.inf)
        p = jax.nn.softmax(s, axis=-1).astype(vv.dtype)
        return jnp.einsum('bhqk,bkhd->bqhd', p, vv)

    qb = q[:, M:].reshape(B, nb, BLOCK, H, dk).swapaxes(0, 1)
    posb = pos[M:].reshape(nb, BLOCK)
    if log_decay is not None:
        meta = attend(q[:, :M], pos[:M], k[:, :M], v[:, :M], pos[:M], log_decay[:, :M], log_decay[:, :M])
        decb = log_decay[:, M:].reshape(B, nb, BLOCK, H).swapaxes(0, 1)
        real = lax.map(lambda a: attend(a[0], a[1], k, v, pos, a[2], log_decay), (qb, posb, decb))
    else:
        meta = attend(q[:, :M], pos[:M], k[:, :M], v[:, :M], pos[:M])
        real = lax.map(lambda a: attend(a[0], a[1], k, v, pos), (qb, posb))
    real = real.swapaxes(0, 1).reshape(B, L - M, H, v.shape[-1])
    return jnp.concatenate([meta, real], axis=1)


def sliding_window_attention(q, k, v, sinks, slopes):
    B, L, Hq, d = q.shape
    Hkv = k.shape[2]
    G = Hq // Hkv
    M = N_META
    nb = (L - M) // BLOCK
    scale = d ** -0.5
    q = q.reshape(B, L, Hkv, G, d)
    sinks = sinks.astype(jnp.float32).reshape(Hkv, G)
    slopes = slopes.reshape(Hkv, G, 1, 1)

    def attend(qq, kk, vv, valid, dist):
        s = jnp.einsum('...qhgd,...khd->...hgqk', qq, kk, preferred_element_type=jnp.float32) * scale - slopes * dist
        s = jnp.where(valid, s, -jnp.inf)
        sink = jnp.broadcast_to(sinks[:, :, None, None], s.shape[:-1] + (1,))
        p = jax.nn.softmax(jnp.concatenate([s, sink], axis=-1), axis=-1)[..., :-1].astype(vv.dtype)
        return jnp.einsum('...hgqk,...khd->...qhgd', p, vv)

    mpos = jnp.arange(M)
    mdist = (mpos[:, None] - mpos[None, :]).astype(jnp.float32)
    meta = attend(q[:, :M], k[:, :M], v[:, :M], mdist >= 0, mdist)

    def band(t):
        tr = t[:, M:].reshape(B, nb, BLOCK, Hkv, d)
        prev = jnp.concatenate([jnp.zeros_like(tr[:, :1]), tr[:, :-1]], axis=1)
        meta_t = jnp.broadcast_to(t[:, None, :M], (B, nb, M, Hkv, d))
        return jnp.concatenate([meta_t, prev, tr], axis=2)

    blk = jnp.arange(nb)
    qpos = M + blk[:, None] * BLOCK + jnp.arange(BLOCK)[None, :]
    band_pos = M + (blk[:, None] - 1) * BLOCK + jnp.arange(2 * BLOCK)[None, :]
    band_dist = qpos[:, :, None] - band_pos[:, None, :]
    band_ok = (band_pos[:, None, :] >= M) & (band_dist >= 0) & (band_dist < WINDOW)
    meta_dist = qpos[:, :, None] - mpos[None, None, :]
    valid = jnp.concatenate([jnp.ones((nb, BLOCK, M), dtype=bool), band_ok], axis=-1)
    dist = jnp.concatenate([meta_dist, band_dist], axis=-1).astype(jnp.float32)
    qr = q[:, M:].reshape(B, nb, BLOCK, Hkv, G, d)
    real = attend(qr, band(k), band(v), valid[:, None, None], dist[:, None, None])
    return jnp.concatenate([meta.reshape(B, M, Hq, d), real.reshape(B, L - M, Hq, d)], axis=1)


def mixer_block(h, norm1_g, w_in, fox_forget_b, fox_q_g, fox_k_g, mla_q_a_g, mla_w_q_up, mla_kv_a_g,
                mla_w_kv_up, mla_q_g, mla_k_g, swa_q_g, swa_k_g, swa_sinks, w_branch, w_o):
    B, L, _ = h.shape
    pos = jnp.arange(L)
    xn = rms_norm(h, norm1_g)
    proj = xn @ w_in
    offsets = [int(o) for o in np.cumsum(IN_SPLITS)[:-1]]
    fq, fk, fv, ff, cq, ckv, krope, sq, sk, sv, gates = jnp.split(proj, offsets, axis=-1)

    fq = rms_norm(fq.reshape(B, L, FOX_HEADS, FOX_HEAD_DIM), fox_q_g)
    fk = rms_norm(fk.reshape(B, L, FOX_HEADS, FOX_HEAD_DIM), fox_k_g)
    fv = fv.reshape(B, L, FOX_HEADS, FOX_HEAD_DIM)
    log_decay = jnp.cumsum(jax.nn.log_sigmoid((ff + fox_forget_b).astype(jnp.float32)), axis=1)
    out_a = causal_block_attention(fq, fk, fv, log_decay)

    q = (rms_norm(cq, mla_q_a_g) @ mla_w_q_up).reshape(B, L, MLA_HEADS, MLA_NOPE_DIM + MLA_ROPE_DIM)
    q = rms_norm(q, mla_q_g)
    q = jnp.concatenate([q[..., :MLA_NOPE_DIM], rope(q[..., MLA_NOPE_DIM:], pos)], axis=-1)
    kv = (rms_norm(ckv, mla_kv_a_g) @ mla_w_kv_up).reshape(B, L, MLA_HEADS, MLA_NOPE_DIM + MLA_V_DIM)
    k_nope, v_mla = kv[..., :MLA_NOPE_DIM], kv[..., MLA_NOPE_DIM:]
    k = jnp.concatenate([k_nope, jnp.broadcast_to(krope[:, :, None, :], (B, L, MLA_HEADS, MLA_ROPE_DIM))], axis=-1)
    k = rms_norm(k, mla_k_g)
    k = jnp.concatenate([k[..., :MLA_NOPE_DIM], rope(k[..., MLA_NOPE_DIM:], pos)], axis=-1)
    out_b = causal_block_attention(q, k, v_mla)

    sq = rms_norm(sq.reshape(B, L, SWA_Q_HEADS, SWA_HEAD_DIM), swa_q_g)
    sk = rms_norm(sk.reshape(B, L, SWA_KV_HEADS, SWA_HEAD_DIM), swa_k_g)
    sv = sv.reshape(B, L, SWA_KV_HEADS, SWA_HEAD_DIM)
    out_c = sliding_window_attention(sq, sk, sv, swa_sinks, alibi_slopes(SWA_Q_HEADS))

    branches = jnp.stack([out_a.reshape(B, L, BRANCH_WIDTH), out_b.reshape(B, L, BRANCH_WIDTH),
                          out_c.reshape(B, L, BRANCH_WIDTH)], axis=2)
    g = jax.nn.sigmoid(gates.reshape(B, L, N_BRANCH, D_MODEL))
    y = jnp.einsum('blnc,ncd->blnd', branches, w_branch)
    merged = jnp.sum(g * y, axis=2)
    return h + merged @ w_o


def conv_ffn(h, norm2_g, w_up, conv_w, conv_b, w_down):
    L = h.shape[1]
    u = rms_norm(h, norm2_g) @ w_up
    up = jnp.pad(u, ((0, 0), (CONV_WIDTH - 1, 0), (0, 0)))
    c = conv_b
    for i in range(CONV_WIDTH):
        c = c + conv_w[i] * up[:, i:i + L]
    gate, val = jnp.split(c, 2, axis=-1)
    return h + (jax.nn.silu(gate) * val) @ w_down


def setup_inputs(seed: int = 0) -> dict:
    key = jax.random.key(seed)
    ks = jax.random.split(key, 23)
    f32 = jnp.float32
    nrm = lambda k, shape, s: jax.random.normal(k, shape, f32) * s
    gain = lambda k, shape: 1.0 + 0.1 * jax.random.normal(k, shape, f32)
    return {
        'x': nrm(ks[0], (BATCH, SEQ, D_MODEL), 1.0),
        'meta_tokens': nrm(ks[1], (N_META, D_MODEL), 1.0),
        'norm1_g': gain(ks[2], (DEPTH, D_MODEL)),
        'w_in': nrm(ks[3], (DEPTH, D_MODEL, IN_WIDTH), D_MODEL ** -0.5),
        'fox_forget_b': 3.0 + 0.1 * jax.random.normal(ks[4], (DEPTH, FOX_HEADS), f32),
        'fox_q_g': gain(ks[5], (DEPTH, FOX_HEAD_DIM)),
        'fox_k_g': gain(ks[6], (DEPTH, FOX_HEAD_DIM)),
        'mla_q_a_g': gain(ks[7], (DEPTH, MLA_Q_RANK)),
        'mla_w_q_up': nrm(ks[8], (DEPTH, MLA_Q_RANK, MLA_HEADS * (MLA_NOPE_DIM + MLA_ROPE_DIM)), MLA_Q_RANK ** -0.5),
        'mla_kv_a_g': gain(ks[9], (DEPTH, MLA_KV_RANK)),
        'mla_w_kv_up': nrm(ks[10], (DEPTH, MLA_KV_RANK, MLA_HEADS * (MLA_NOPE_DIM + MLA_V_DIM)), MLA_KV_RANK ** -0.5),
        'mla_q_g': gain(ks[11], (DEPTH, MLA_NOPE_DIM + MLA_ROPE_DIM)),
        'mla_k_g': gain(ks[12], (DEPTH, MLA_NOPE_DIM + MLA_ROPE_DIM)),
        'swa_q_g': gain(ks[13], (DEPTH, SWA_HEAD_DIM)),
        'swa_k_g': gain(ks[14], (DEPTH, SWA_HEAD_DIM)),
        'swa_sinks': nrm(ks[15], (DEPTH, SWA_Q_HEADS), 0.5),
        'w_branch': nrm(ks[16], (DEPTH, N_BRANCH, BRANCH_WIDTH, D_MODEL), BRANCH_WIDTH ** -0.5),
        'w_o': nrm(ks[17], (DEPTH, D_MODEL, D_MODEL), D_MODEL ** -0.5),
        'norm2_g': gain(ks[18], (DEPTH, D_MODEL)),
        'ffn_w_up': nrm(ks[19], (DEPTH, D_MODEL, 2 * D_FF), D_MODEL ** -0.5),
        'ffn_conv_w': nrm(ks[20], (DEPTH, CONV_WIDTH, 2 * D_FF), CONV_WIDTH ** -0.5),
        'ffn_conv_b': nrm(ks[21], (DEPTH, 2 * D_FF), 0.02),
        'ffn_w_down': nrm(ks[22], (DEPTH, D_FF, D_MODEL), D_FF ** -0.5),
    }


def reference(x, meta_tokens, norm1_g, w_in, fox_forget_b, fox_q_g, fox_k_g, mla_q_a_g, mla_w_q_up,
              mla_kv_a_g, mla_w_kv_up, mla_q_g, mla_k_g, swa_q_g, swa_k_g, swa_sinks, w_branch, w_o,
              norm2_g, ffn_w_up, ffn_conv_w, ffn_conv_b, ffn_w_down):
    B = x.shape[0]
    meta = jnp.broadcast_to(meta_tokens[None].astype(x.dtype), (B, N_META, x.shape[-1]))
    h = jnp.concatenate([meta, x], axis=1)
    for l in range(DEPTH):
        h = mixer_block(h, norm1_g[l], w_in[l], fox_forget_b[l], fox_q_g[l], fox_k_g[l], mla_q_a_g[l],
                        mla_w_q_up[l], mla_kv_a_g[l], mla_w_kv_up[l], mla_q_g[l], mla_k_g[l], swa_q_g[l],
                        swa_k_g[l], swa_sinks[l], w_branch[l], w_o[l])
        h = conv_ffn(h, norm2_g[l], ffn_w_up[l], ffn_conv_w[l], ffn_conv_b[l], ffn_w_down[l])
    return h[:, N_META:]
```

```python
import functools
import math

import numpy as np
import jax
import jax.numpy as jnp
from jax import lax
from jax.experimental import pallas as pl
from jax.experimental.pallas import tpu as pltpu

F32 = jnp.float32
BF16 = jnp.bfloat16

N_META = 16
EPS = 1e-6
N_HEADS = 8
HEAD_DIM = 64
ROPE_DIM = 32
MLA_Q_RANK = 256
MLA_KV_RANK = 128
SWA_KV_HEADS = 2
WINDOW = 128
ROPE_THETA = 10000.0
BRANCH_WIDTH = N_HEADS * HEAD_DIM
N_BRANCH = 3
CONV_WIDTH = 3

LANES = 128
FRONT = 128
PAD = FRONT - N_META
NEG = -1e30
DEC_PARTS = 3
VMEM_LIMIT = 56 * 1024 * 1024

_A_FQ, _A_FK, _A_FV, _A_SQ = 0, 512, 1024, 1536
_A_SK, _A_SV, _A_CQ, _A_CKV, _A_KR, _A_FF = 2048, 2176, 2304, 2560, 2688, 2816
_A_WIDTH = 2944


def _row_tile(lp):
    best = 128
    for t in range(128, 513, 128):
        if lp % t == 0:
            best = t
    return best


def _lane_iota(shape):
    return lax.broadcasted_iota(jnp.int32, shape, len(shape) - 1)


def _rms_rows(y, width):
    ss = jnp.sum(y * y, axis=-1, keepdims=True)
    return y * lax.rsqrt(ss * (1.0 / width) + EPS)


def _half_sums(y2):
    lane = _lane_iota(y2.shape)
    lo = jnp.sum(jnp.where(lane < HEAD_DIM, y2, 0.0), axis=-1, keepdims=True)
    hi = jnp.sum(jnp.where(lane >= HEAD_DIM, y2, 0.0), axis=-1, keepdims=True)
    return lo, hi


def _headnorm64_tile(yt):
    lo, hi = _half_sums(yt * yt)
    lane = _lane_iota(yt.shape)
    r = jnp.where(lane < HEAD_DIM, lax.rsqrt(lo * (1.0 / HEAD_DIM) + EPS), lax.rsqrt(hi * (1.0 / HEAD_DIM) + EPS))
    return yt * r


def _rotate_half(x):
    lane = _lane_iota(x.shape)
    half = ROPE_DIM // 2
    return jnp.where((lane % ROPE_DIM) < half, pltpu.roll(x, LANES - half, 1), pltpu.roll(x, half, 1))


def _quad_select(lane, vals):
    g = lane // ROPE_DIM
    out = jnp.where(g == 0, vals[0], vals[1])
    out = jnp.where(g == 2, vals[2], out)
    return jnp.where(g == 3, vals[3], out)


def _in_proj_kernel(h_ref, g1_ref, wa_ref, gfq_ref, gfk_ref, gsq_ref, gsk_ref, fb_ref,
                    gcq_ref, wqu_ref, gckv_ref, wkvu_ref, gmq_n_ref, gmq_r_ref, gmk_n_ref, gmk_r_ref,
                    cos_ref, sin_ref, tri_ref, eq_ref, ek_ref, qconst_ref, kconst_ref,
                    qf_ref, kf_ref, vf_ref, qm_ref, km_ref, vm_ref, sq_ref, sk_ref, sv_ref,
                    carry_ref, *, tm):
    j = pl.program_id(1)
    h = h_ref[...]
    xn = (_rms_rows(h, h.shape[-1]) * g1_ref[...]).astype(BF16)

    def proj(c0, width):
        return jnp.dot(xn, wa_ref[:, c0:c0 + width], preferred_element_type=F32)

    yq = proj(_A_FQ, 512)
    yk = proj(_A_FK, 512)
    for t in range(4):
        sl = slice(t * LANES, (t + 1) * LANES)
        qf_ref[:, 2 * t * LANES:(2 * t + 1) * LANES] = (_headnorm64_tile(yq[:, sl]) * gfq_ref[:, sl]).astype(BF16)
        kf_ref[:, 2 * t * LANES:(2 * t + 1) * LANES] = (_headnorm64_tile(yk[:, sl]) * gfk_ref[:, sl]).astype(BF16)
    vf_ref[...] = proj(_A_FV, 512).astype(BF16)

    @pl.when(j == 0)
    def _():
        carry_ref[...] = jnp.zeros_like(carry_ref)

    z = proj(_A_FF, LANES) + fb_ref[...]
    ls = jnp.minimum(z, 0.0) - jnp.log1p(jnp.exp(-jnp.abs(z)))
    row = j * tm + lax.broadcasted_iota(jnp.int32, ls.shape, 0)
    ls = jnp.where((_lane_iota(ls.shape) < N_HEADS) & (row >= PAD), ls, 0.0)
    tri = tri_ref[...]
    c = carry_ref[0:1, :]
    rem = ls
    for _ in range(DEC_PARTS):
        part = rem.astype(BF16)
        c = c + jnp.dot(tri, part, preferred_element_type=F32)
        rem = rem - part.astype(F32)
    carry_ref[0:1, :] = c[tm - 1:tm, :]
    qd = qconst_ref[...]
    kd = kconst_ref[...]
    rem = c
    for i in range(DEC_PARTS):
        part = rem.astype(BF16)
        qd = qd + jnp.dot(part, eq_ref[i], preferred_element_type=F32)
        kd = kd + jnp.dot(part, ek_ref[i], preferred_element_type=F32)
        rem = rem - part.astype(F32)
    qd = qd.astype(BF16)
    kd = kd.astype(BF16)
    for t in range(4):
        qf_ref[:, (2 * t + 1) * LANES:(2 * t + 2) * LANES] = qd
        kf_ref[:, (2 * t + 1) * LANES:(2 * t + 2) * LANES] = kd

    ysq = proj(_A_SQ, 512)
    for t in range(4):
        sl = slice(t * LANES, (t + 1) * LANES)
        sq_ref[:, sl] = (_headnorm64_tile(ysq[:, sl]) * gsq_ref[:, sl]).astype(BF16)
    sk_ref[...] = (_headnorm64_tile(proj(_A_SK, LANES)) * gsk_ref[...]).astype(BF16)
    sv_ref[...] = proj(_A_SV, LANES).astype(BF16)

    cos = cos_ref[...]
    sin = sin_ref[...]

    def rope(x):
        return x * cos + _rotate_half(x) * sin

    cq = (_rms_rows(proj(_A_CQ, MLA_Q_RANK), MLA_Q_RANK) * gcq_ref[...]).astype(BF16)
    yq = jnp.dot(cq, wqu_ref[...], preferred_element_type=F32)
    lane = _lane_iota((tm, LANES))
    quad_id = lane // ROPE_DIM
    ss = []
    for t in range(4):
        lo, hi = _half_sums(jnp.square(yq[:, t * LANES:(t + 1) * LANES]))
        ss += [lo, hi]
    for u in range(2):
        y2 = jnp.square(yq[:, 512 + u * LANES:512 + (u + 1) * LANES])
        for g in range(4):
            ss[4 * u + g] = ss[4 * u + g] + jnp.sum(jnp.where(quad_id == g, y2, 0.0), axis=-1, keepdims=True)
    width = HEAD_DIM + ROPE_DIM
    rq = [lax.rsqrt(s * (1.0 / width) + EPS) for s in ss]
    q_rope = []
    for u in range(2):
        sl = slice(512 + u * LANES, 512 + (u + 1) * LANES)
        x = yq[:, sl] * _quad_select(lane, rq[4 * u:4 * u + 4]) * gmq_r_ref[...]
        q_rope.append(rope(x).astype(BF16))
    for t in range(4):
        sl = slice(t * LANES, (t + 1) * LANES)
        r = jnp.where(lane < HEAD_DIM, rq[2 * t], rq[2 * t + 1])
        qm_ref[:, 2 * t * LANES:(2 * t + 1) * LANES] = (yq[:, sl] * r * gmq_n_ref[:, sl]).astype(BF16)
        qm_ref[:, (2 * t + 1) * LANES:(2 * t + 2) * LANES] = q_rope[t // 2]

    ckv = (_rms_rows(proj(_A_CKV, MLA_KV_RANK), MLA_KV_RANK) * gckv_ref[...]).astype(BF16)
    ykv = jnp.dot(ckv, wkvu_ref[...], preferred_element_type=F32)
    kr4 = proj(_A_KR, LANES)
    ss_rope = jnp.sum(kr4 * kr4, axis=-1, keepdims=True) * 0.25
    rk = []
    for t in range(4):
        lo, hi = _half_sums(jnp.square(ykv[:, t * LANES:(t + 1) * LANES]))
        rk += [lax.rsqrt((lo + ss_rope) * (1.0 / width) + EPS), lax.rsqrt((hi + ss_rope) * (1.0 / width) + EPS)]
    kr_base = rope(kr4 * gmk_r_ref[...])
    k_rope = [(kr_base * _quad_select(lane, rk[4 * u:4 * u + 4])).astype(BF16) for u in range(2)]
    for t in range(4):
        sl = slice(t * LANES, (t + 1) * LANES)
        r = jnp.where(lane < HEAD_DIM, rk[2 * t], rk[2 * t + 1])
        km_ref[:, 2 * t * LANES:(2 * t + 1) * LANES] = (ykv[:, sl] * r * gmk_n_ref[:, sl]).astype(BF16)
        km_ref[:, (2 * t + 1) * LANES:(2 * t + 2) * LANES] = k_rope[t // 2]
    vm_ref[...] = ykv[:, 512:1024].astype(BF16)


def _in_proj(h, p, consts, tm):
    b, lp, d = h.shape
    nt = lp // tm
    row_blk = lambda w: pl.BlockSpec((None, tm, w), lambda bi, j: (bi, j, 0))
    full = lambda a: pl.BlockSpec(a.shape, lambda bi, j: (0,) * a.ndim)
    tab = pl.BlockSpec((tm, LANES), lambda bi, j: (j, 0))
    ins = [h, p['g1'], p['wa'], p['gfq'], p['gfk'], p['gsq'], p['gsk'], p['fb'],
           p['gcq'], p['wqu'], p['gckv'], p['wkvu'], p['gmq_n'], p['gmq_r'], p['gmk_n'], p['gmk_r']]
    in_specs = [row_blk(d)] + [full(a) for a in ins[1:]]
    ins += [consts['cos'], consts['sin'], consts['tri'], consts['eq'], consts['ek'], consts['qconst'], consts['kconst']]
    in_specs += [tab, tab] + [full(consts[k]) for k in ('tri', 'eq', 'ek', 'qconst', 'kconst')]
    widths = [1024, 1024, 512, 1024, 1024, 512, 512, 128, 128]
    out_shape = [jax.ShapeDtypeStruct((b, lp, w), BF16) for w in widths]
    out_specs = [row_blk(w) for w in widths]
    return pl.pallas_call(
        functools.partial(_in_proj_kernel, tm=tm),
        grid=(b, nt),
        in_specs=in_specs,
        out_specs=out_specs,
        out_shape=out_shape,
        scratch_shapes=[pltpu.VMEM((8, LANES), F32)],
        compiler_params=pltpu.CompilerParams(dimension_semantics=("arbitrary", "arbitrary"),
                                             vmem_limit_bytes=VMEM_LIMIT),
        name="in_proj",
    )(*ins)


def _causal_attn_kernel(mask_ref, q_ref, k_ref, v_ref, o_ref, m_ref, l_ref, acc_ref, *, tq, n_blocks):
    masks = mask_ref[...]
    nt = (((1,), (1,)), ((), ()))

    def init(n):
        m_ref[:, 0:n, :] = jnp.full((2, n, 1), NEG, F32)
        l_ref[:, 0:n, :] = jnp.zeros((2, n, 1), F32)
        acc_ref[:, 0:n, :] = jnp.zeros((2, n, LANES), F32)

    def chunk(q, n, k0, tk, mask_fn):
        k = k_ref[pl.ds(k0, tk), :]
        v = v_ref[pl.ds(k0, tk), :]
        for a in range(2):
            s = lax.dot_general(q * masks[a:a + 1, :], k, nt, preferred_element_type=F32)
            if mask_fn is not None:
                s = jnp.where(mask_fn(s.shape), s, NEG)
            m_prev = m_ref[a, 0:n, :]
            m_new = jnp.maximum(m_prev, jnp.max(s, axis=-1, keepdims=True))
            alpha = jnp.exp(m_prev - m_new)
            p = jnp.exp(s - m_new)
            l_ref[a, 0:n, :] = alpha * l_ref[a, 0:n, :] + jnp.sum(p, axis=-1, keepdims=True)
            acc_ref[a, 0:n, :] = alpha * acc_ref[a, 0:n, :] + jnp.dot(p.astype(BF16), v, preferred_element_type=F32)
            m_ref[a, 0:n, :] = m_new

    def finish(r0, n):
        oa = acc_ref[0, 0:n, :] * (1.0 / l_ref[0, 0:n, :])
        ob = acc_ref[1, 0:n, :] * (1.0 / l_ref[1, 0:n, :])
        o_ref[pl.ds(r0, n), :] = jnp.where(_lane_iota(oa.shape) < HEAD_DIM, oa, ob).astype(o_ref.dtype)

    def causal(shape):
        return lax.broadcasted_iota(jnp.int32, shape, 1) <= lax.broadcasted_iota(jnp.int32, shape, 0)

    def not_pad(shape):
        return lax.broadcasted_iota(jnp.int32, shape, 1) >= PAD

    init(FRONT)
    chunk(q_ref[0:FRONT, :], FRONT, 0, FRONT, lambda shape: causal(shape) & not_pad(shape))
    finish(0, FRONT)

    def q_block(i, carry):
        r0 = pl.multiple_of(FRONT + i * tq, LANES)
        q = q_ref[pl.ds(r0, tq), :]
        init(tq)
        chunk(q, tq, 0, FRONT, not_pad)

        def kv_block(jj, c):
            chunk(q, tq, pl.multiple_of(FRONT + jj * tq, LANES), tq, None)
            return c

        lax.fori_loop(0, i, kv_block, 0)
        chunk(q, tq, r0, tq, causal)
        finish(r0, tq)
        return carry

    lax.fori_loop(0, n_blocks, q_block, 0)


def _causal_attn(q, k, v, masks, aux_block, tq):
    b, lp, _ = v.shape
    n_pairs = N_HEADS // 2
    n_blocks = (lp - FRONT) // tq

    def qk_spec():
        return pl.BlockSpec((None, lp, 2 * LANES), lambda bi, p: (bi, 0, p))

    return pl.pallas_call(
        functools.partial(_causal_attn_kernel, tq=tq, n_blocks=n_blocks),
        grid=(b, n_pairs),
        in_specs=[pl.BlockSpec((None, 2, 2 * LANES), lambda bi, p: (p, 0, 0)),
                  qk_spec(), qk_spec(),
                  pl.BlockSpec((None, lp, LANES), lambda bi, p: (bi, 0, p))],
        out_specs=pl.BlockSpec((None, lp, LANES), lambda bi, p: (bi, 0, p)),
        out_shape=jax.ShapeDtypeStruct((b, lp, BRANCH_WIDTH), BF16),
        scratch_shapes=[pltpu.VMEM((2, tq, 1), F32), pltpu.VMEM((2, tq, 1), F32), pltpu.VMEM((2, tq, LANES), F32)],
        compiler_params=pltpu.CompilerParams(dimension_semantics=("arbitrary", "arbitrary"),
                                             vmem_limit_bytes=VMEM_LIMIT),
        name="causal_attn",
    )(masks, q, k, v)


def _swa_kernel(sinks_ref, q_ref, k_ref, v_ref, o_ref):
    i = pl.program_id(1)
    r0 = i * FRONT
    start = pl.multiple_of(jnp.maximum(r0 - WINDOW, 0), LANES)
    kk = jnp.concatenate([k_ref[0:FRONT, :], k_ref[pl.ds(start, 2 * WINDOW), :]], axis=0)
    vv = jnp.concatenate([v_ref[0:FRONT, :], v_ref[pl.ds(start, 2 * WINDOW), :]], axis=0)
    shape = (FRONT, FRONT + 2 * WINDOW)
    row = lax.broadcasted_iota(jnp.int32, shape, 0)
    col = lax.broadcasted_iota(jnp.int32, shape, 1)
    qpos = r0 + row
    in_front = col < FRONT
    kpos = jnp.where(in_front, col, start + col - FRONT)
    lowest = jnp.where(in_front, PAD, jnp.maximum(qpos - (WINDOW - 1), FRONT))
    valid = (kpos >= lowest) & (kpos <= qpos)
    dist = (qpos - kpos).astype(F32)
    lane = _lane_iota((FRONT, LANES))
    lane_q = _lane_iota((1, LANES))
    nt = (((1,), (1,)), ((), ()))
    for t in range(4):
        qt = q_ref[:, t * LANES:(t + 1) * LANES]
        outs = []
        for a in range(2):
            head = t + 4 * a
            slope = 2.0 ** (-8.0 * (head + 1) / N_HEADS)
            keep = ((lane_q >= HEAD_DIM) if a else (lane_q < HEAD_DIM)).astype(BF16)
            s = lax.dot_general(qt * keep, kk, nt, preferred_element_type=F32) - slope * dist
            s = jnp.where(valid, s, NEG)
            sink = sinks_ref[head]
            m = jnp.maximum(jnp.max(s, axis=-1, keepdims=True), sink)
            p = jnp.exp(s - m)
            den = jnp.sum(p, axis=-1, keepdims=True) + jnp.exp(sink - m)
            outs.append(jnp.dot(p.astype(BF16), vv, preferred_element_type=F32) * (1.0 / den))
        o_ref[:, t * LANES:(t + 1) * LANES] = jnp.where(lane < HEAD_DIM, outs[0], outs[1]).astype(o_ref.dtype)


def _swa_attn(q, k, v, sinks):
    b, lp, _ = q.shape
    return pl.pallas_call(
        _swa_kernel,
        grid=(b, lp // FRONT),
        in_specs=[pl.BlockSpec(memory_space=pltpu.SMEM),
                  pl.BlockSpec((None, FRONT, BRANCH_WIDTH), lambda bi, i: (bi, i, 0)),
                  pl.BlockSpec((None, lp, LANES), lambda bi, i: (bi, 0, 0)),
                  pl.BlockSpec((None, lp, LANES), lambda bi, i: (bi, 0, 0))],
        out_specs=pl.BlockSpec((None, FRONT, BRANCH_WIDTH), lambda bi, i: (bi, i, 0)),
        out_shape=jax.ShapeDtypeStruct((b, lp, BRANCH_WIDTH), BF16),
        compiler_params=pltpu.CompilerParams(dimension_semantics=("arbitrary", "arbitrary"),
                                             vmem_limit_bytes=VMEM_LIMIT),
        name="swa_attn",
    )(sinks, q, k, v)


def _merge_kernel(h_ref, oa_ref, ob_ref, oc_ref, g1_ref, wg_ref, wb_ref, wo_ref, out_ref):
    h = h_ref[...]
    d = h.shape[-1]
    xn = (_rms_rows(h, d) * g1_ref[...]).astype(BF16)
    merged = None
    for n, o_ref in enumerate((oa_ref, ob_ref, oc_ref)):
        gate = jax.nn.sigmoid(jnp.dot(xn, wg_ref[:, n * d:(n + 1) * d], preferred_element_type=F32))
        y = jnp.dot(o_ref[...], wb_ref[n], preferred_element_type=F32)
        merged = gate * y if merged is None else merged + gate * y
    out_ref[...] = h + jnp.dot(merged.astype(BF16), wo_ref[...], preferred_element_type=F32)


def _merge(h, oa, ob, oc, p, tm):
    b, lp, d = h.shape
    row_blk = lambda w: pl.BlockSpec((None, tm, w), lambda bi, j: (bi, j, 0))
    full = lambda a: pl.BlockSpec(a.shape, lambda bi, j: (0,) * a.ndim)
    ws = [p['g1'], p['wg'], p['wb'], p['wo']]
    return pl.pallas_call(
        _merge_kernel,
        grid=(b, lp // tm),
        in_specs=[row_blk(d), row_blk(BRANCH_WIDTH), row_blk(BRANCH_WIDTH), row_blk(BRANCH_WIDTH)] + [full(a) for a in ws],
        out_specs=row_blk(d),
        out_shape=jax.ShapeDtypeStruct(h.shape, F32),
        compiler_params=pltpu.CompilerParams(dimension_semantics=("arbitrary", "arbitrary"),
                                             vmem_limit_bytes=VMEM_LIMIT),
        name="merge",
    )(h, oa, ob, oc, *ws)


_FF_CHUNK = 256
_HALO = 8


def _ffn_kernel(h_ref, g2_ref, wup_ref, cw_ref, cb_ref, wdn_ref, out_ref, carry_ref, ubuf_ref, act_ref, *, tm, d_ff):
    j = pl.program_id(1)
    h = h_ref[...]
    xn = _rms_rows(h, h.shape[-1]) * g2_ref[...]
    row = j * tm + lax.broadcasted_iota(jnp.int32, (tm, 1), 0)
    xn = jnp.where(row >= PAD, xn, 0.0).astype(BF16)

    @pl.when(j == 0)
    def _():
        carry_ref[...] = jnp.zeros_like(carry_ref)

    for c in range(d_ff // _FF_CHUNK):
        acts = []
        for half in range(2):
            c0 = half * d_ff + c * _FF_CHUNK
            cols = slice(c0, c0 + _FF_CHUNK)
            u = jnp.dot(xn, wup_ref[:, cols], preferred_element_type=F32)
            ubuf_ref[half, 0:_HALO, :] = carry_ref[:, cols]
            ubuf_ref[half, _HALO:_HALO + tm, :] = u
            carry_ref[:, cols] = u[tm - _HALO:tm, :]
            u1 = ubuf_ref[half, _HALO - 1:_HALO - 1 + tm, :]
            u2 = ubuf_ref[half, _HALO - 2:_HALO - 2 + tm, :]
            acts.append(cb_ref[:, cols] + cw_ref[2:3, cols] * u + cw_ref[1:2, cols] * u1 + cw_ref[0:1, cols] * u2)
        gate, val = acts
        act_ref[:, c * _FF_CHUNK:(c + 1) * _FF_CHUNK] = (gate * jax.nn.sigmoid(gate) * val).astype(BF16)
    out_ref[...] = h + jnp.dot(act_ref[...], wdn_ref[...], preferred_element_type=F32)


def _ffn(h, p, tm):
    b, lp, d = h.shape
    d_ff = p['wdn'].shape[0]
    row_blk = pl.BlockSpec((None, tm, d), lambda bi, j: (bi, j, 0))
    full = lambda a: pl.BlockSpec(a.shape, lambda bi, j: (0,) * a.ndim, pipeline_mode=pl.Buffered(1))
    ws = [p['g2'], p['wup'], p['cw'], p['cb'], p['wdn']]
    return pl.pallas_call(
        functools.partial(_ffn_kernel, tm=tm, d_ff=d_ff),
        grid=(b, lp // tm),
        in_specs=[row_blk] + [full(a) for a in ws],
        out_specs=row_blk,
        out_shape=jax.ShapeDtypeStruct(h.shape, F32),
        scratch_shapes=[pltpu.VMEM((_HALO, 2 * d_ff), F32),
                        pltpu.VMEM((2, _HALO + tm, _FF_CHUNK), F32),
                        pltpu.VMEM((tm, d_ff), BF16)],
        compiler_params=pltpu.CompilerParams(dimension_semantics=("arbitrary", "arbitrary"),
                                             vmem_limit_bytes=VMEM_LIMIT),
        name="conv_ffn",
    )(h, *ws)


def _constants(lp, tm):
    half = ROPE_DIM // 2
    pos = (np.arange(lp) - PAD).astype(np.float32)
    freqs = (ROPE_THETA ** (-np.arange(half, dtype=np.float32) / half)).astype(np.float32)
    lane = np.arange(LANES)
    ang = jnp.asarray(pos)[:, None] * jnp.asarray(freqs[lane % half])[None, :]
    sign = np.where((lane % ROPE_DIM) < half, -1.0, 1.0).astype(np.float32)
    tri = np.tril(np.ones((tm, tm), np.float32))
    eq = np.zeros((DEC_PARTS, LANES, LANES), np.float32)
    ek = np.zeros((DEC_PARTS, LANES, LANES), np.float32)
    qconst = np.zeros((1, LANES), np.float32)
    kconst = np.zeros((1, LANES), np.float32)
    for hd in range(N_HEADS):
        base = 2 * DEC_PARTS * hd
        for i in range(DEC_PARTS):
            eq[i, hd, base + i] = 1.0
            ek[i, hd, base + DEC_PARTS + i] = -1.0
            qconst[0, base + DEC_PARTS + i] = 1.0
            kconst[0, base + i] = 1.0
    fox_masks = np.zeros((N_HEADS // 2, 2, 2 * LANES), np.float32)
    mla_masks = np.zeros((N_HEADS // 2, 2, 2 * LANES), np.float32)
    for p in range(N_HEADS // 2):
        for a in range(2):
            hd = 2 * p + a
            fox_masks[p, a, a * HEAD_DIM:(a + 1) * HEAD_DIM] = 1.0
            fox_masks[p, a, LANES + 2 * DEC_PARTS * hd:LANES + 2 * DEC_PARTS * (hd + 1)] = 1.0
            mla_masks[p, a, a * HEAD_DIM:(a + 1) * HEAD_DIM] = 1.0
            g = hd % 4
            mla_masks[p, a, LANES + g * ROPE_DIM:LANES + (g + 1) * ROPE_DIM] = 1.0
    return {
        'cos': jnp.cos(ang), 'sin': jnp.sin(ang) * jnp.asarray(sign)[None, :],
        'tri': jnp.asarray(tri, BF16), 'eq': jnp.asarray(eq, BF16), 'ek': jnp.asarray(ek, BF16),
        'qconst': jnp.asarray(qconst), 'kconst': jnp.asarray(kconst),
        'fox_masks': jnp.asarray(fox_masks, BF16), 'mla_masks': jnp.asarray(mla_masks, BF16),
    }


def _swa_head_perm():
    cols = []
    for t in range(4):
        for hd in (t, 4 + t):
            cols += list(range(hd * HEAD_DIM, (hd + 1) * HEAD_DIM))
    return np.asarray(cols)


def _layer_params(l, norm1_g, w_in, fox_forget_b, fox_q_g, fox_k_g, mla_q_a_g, mla_w_q_up, mla_kv_a_g,
                  mla_w_kv_up, mla_q_g, mla_k_g, swa_q_g, swa_k_g, swa_sinks, w_branch, w_o,
                  norm2_g, ffn_w_up, ffn_conv_w, ffn_conv_b, ffn_w_down):
    d = w_in.shape[1]
    w = w_in[l]
    o_fq, o_fk, o_fv, o_ff = 0, 512, 1024, 1536
    o_cq = o_ff + N_HEADS
    o_ckv = o_cq + MLA_Q_RANK
    o_kr = o_ckv + MLA_KV_RANK
    o_sq = o_kr + ROPE_DIM
    o_sk = o_sq + 512
    o_sv = o_sk + SWA_KV_HEADS * HEAD_DIM
    o_g = o_sv + SWA_KV_HEADS * HEAD_DIM
    perm = _swa_head_perm()
    wa = jnp.concatenate([
        w[:, o_fq:o_fq + 512], w[:, o_fk:o_fk + 512], w[:, o_fv:o_fv + 512],
        w[:, o_sq:o_sq + 512][:, perm], w[:, o_sk:o_sk + 128], w[:, o_sv:o_sv + 128],
        w[:, o_cq:o_cq + MLA_Q_RANK], w[:, o_ckv:o_ckv + MLA_KV_RANK],
        jnp.tile(w[:, o_kr:o_kr + ROPE_DIM], (1, LANES // ROPE_DIM)),
        w[:, o_ff:o_ff + N_HEADS], jnp.zeros((d, LANES - N_HEADS), w.dtype)], axis=1).astype(BF16)
    tile_row = lambda g, n: jnp.tile(g, n)[None, :].astype(F32)
    width = HEAD_DIM + ROPE_DIM
    qcols = np.concatenate([np.concatenate([np.arange(hd * width, hd * width + HEAD_DIM) for hd in range(N_HEADS)]),
                            np.concatenate([np.arange(hd * width + HEAD_DIM, (hd + 1) * width) for hd in range(N_HEADS)])])
    kvcols = np.concatenate([np.concatenate([np.arange(hd * 2 * HEAD_DIM, hd * 2 * HEAD_DIM + HEAD_DIM) for hd in range(N_HEADS)]),
                             np.concatenate([np.arange(hd * 2 * HEAD_DIM + HEAD_DIM, (hd + 1) * 2 * HEAD_DIM) for hd in range(N_HEADS)])])
    fb = jnp.concatenate([fox_forget_b[l], jnp.zeros((LANES - N_HEADS,), F32)])[None, :]
    wb = w_branch[l]
    wb = jnp.stack([wb[0], wb[1], wb[2][perm, :]]).astype(BF16)
    return {
        'g1': norm1_g[l][None, :], 'wa': wa,
        'gfq': tile_row(fox_q_g[l], N_HEADS) * (HEAD_DIM ** -0.5), 'gfk': tile_row(fox_k_g[l], N_HEADS),
        'gsq': tile_row(swa_q_g[l], N_HEADS) * (HEAD_DIM ** -0.5), 'gsk': tile_row(swa_k_g[l], SWA_KV_HEADS),
        'fb': fb,
        'gcq': mla_q_a_g[l][None, :], 'wqu': mla_w_q_up[l][:, qcols].astype(BF16),
        'gckv': mla_kv_a_g[l][None, :], 'wkvu': mla_w_kv_up[l][:, kvcols].astype(BF16),
        'gmq_n': tile_row(mla_q_g[l][:HEAD_DIM], N_HEADS) * (width ** -0.5),
        'gmq_r': tile_row(mla_q_g[l][HEAD_DIM:], LANES // ROPE_DIM) * (width ** -0.5),
        'gmk_n': tile_row(mla_k_g[l][:HEAD_DIM], N_HEADS),
        'gmk_r': tile_row(mla_k_g[l][HEAD_DIM:], LANES // ROPE_DIM),
        'sinks': swa_sinks[l].astype(F32),
        'wg': w[:, o_g:].astype(BF16), 'wb': wb, 'wo': w_o[l].astype(BF16),
        'g2': norm2_g[l][None, :], 'wup': ffn_w_up[l].astype(BF16), 'cw': ffn_conv_w[l], 'cb': ffn_conv_b[l][None, :],
        'wdn': ffn_w_down[l].astype(BF16),
    }


def kernel(x, meta_tokens, norm1_g, w_in, fox_forget_b, fox_q_g, fox_k_g, mla_q_a_g, mla_w_q_up, mla_kv_a_g, mla_w_kv_up, mla_q_g, mla_k_g, swa_q_g, swa_k_g, swa_sinks, w_branch, w_o, norm2_g, ffn_w_up, ffn_conv_w, ffn_conv_b, ffn_w_down):
    b, seq, d = x.shape
    assert seq % LANES == 0 and meta_tokens.shape == (N_META, d)
    lp = FRONT + seq
    tm = _row_tile(lp)
    tq = _row_tile(seq)
    consts = _constants(lp, tm)
    front = jnp.concatenate([jnp.zeros((PAD, d), x.dtype), meta_tokens.astype(x.dtype)], axis=0)
    h = jnp.concatenate([jnp.broadcast_to(front[None], (b, FRONT, d)), x], axis=1)
    weights = (norm1_g, w_in, fox_forget_b, fox_q_g, fox_k_g, mla_q_a_g, mla_w_q_up, mla_kv_a_g, mla_w_kv_up,
               mla_q_g, mla_k_g, swa_q_g, swa_k_g, swa_sinks, w_branch, w_o, norm2_g, ffn_w_up, ffn_conv_w,
               ffn_conv_b, ffn_w_down)
    for l in range(w_in.shape[0]):
        p = _layer_params(l, *weights)
        qf, kf, vf, qm, km, vm, sq, sk, sv = _in_proj(h, p, consts, tm)
        out_a = _causal_attn(qf, kf, vf, consts['fox_masks'], lambda pr: 2 * pr + 1, tq)
        out_b = _causal_attn(qm, km, vm, consts['mla_masks'], lambda pr: 2 * pr + 1, tq)
        out_c = _swa_attn(sq, sk, sv, p['sinks'])
        h = _merge(h, out_a, out_b, out_c, p, tm)
        h = _ffn(h, p, tm)
    return h[:, FRONT:]
```

```python
import functools
import math

import numpy as np
import jax
import jax.numpy as jnp
from jax import lax
from jax.experimental import pallas as pl
from jax.experimental.pallas import tpu as pltpu

F32 = jnp.float32
BF16 = jnp.bfloat16

N_META = 16
EPS = 1e-6
N_HEADS = 8
HEAD_DIM = 64
ROPE_DIM = 32
MLA_Q_RANK = 256
MLA_KV_RANK = 128
SWA_KV_HEADS = 2
WINDOW = 128
ROPE_THETA = 10000.0
BRANCH_WIDTH = N_HEADS * HEAD_DIM
N_BRANCH = 3
CONV_WIDTH = 3

LANES = 128
FRONT = 128
PAD = FRONT - N_META
NEG = -1e30
DEC_PARTS = 3
LOG2E = math.log2(math.e)
ONES_ROWS = 16
VT_ROWS = HEAD_DIM + ONES_ROWS
QUERY_SUB = 256
VMEM_LIMIT = 56 * 1024 * 1024

_A_FQ, _A_FK, _A_FV, _A_SQ = 0, 512, 1024, 1536
_A_SK, _A_SV, _A_CQ, _A_CKV, _A_KR, _A_FF = 2048, 2176, 2304, 2560, 2688, 2816
_A_WIDTH = 2944


def _row_tile(lp):
    best = 128
    for t in range(128, 513, 128):
        if lp % t == 0:
            best = t
    return best


def _lane_iota(shape):
    return lax.broadcasted_iota(jnp.int32, shape, len(shape) - 1)


def _rms_rows(y, width):
    ss = jnp.sum(y * y, axis=-1, keepdims=True)
    return y * lax.rsqrt(ss * (1.0 / width) + EPS)


def _half_sums(y2):
    lane = _lane_iota(y2.shape)
    lo = jnp.sum(jnp.where(lane < HEAD_DIM, y2, 0.0), axis=-1, keepdims=True)
    hi = jnp.sum(jnp.where(lane >= HEAD_DIM, y2, 0.0), axis=-1, keepdims=True)
    return lo, hi


def _headnorm64_tile(yt):
    lo, hi = _half_sums(yt * yt)
    lane = _lane_iota(yt.shape)
    r = jnp.where(lane < HEAD_DIM, lax.rsqrt(lo * (1.0 / HEAD_DIM) + EPS), lax.rsqrt(hi * (1.0 / HEAD_DIM) + EPS))
    return yt * r


def _rotate_half(x):
    lane = _lane_iota(x.shape)
    half = ROPE_DIM // 2
    return jnp.where((lane % ROPE_DIM) < half, pltpu.roll(x, LANES - half, 1), pltpu.roll(x, half, 1))


def _quad_select(lane, vals):
    g = lane // ROPE_DIM
    out = jnp.where(g == 0, vals[0], vals[1])
    out = jnp.where(g == 2, vals[2], out)
    return jnp.where(g == 3, vals[3], out)


def _in_proj_kernel(h_ref, g1_ref, wa_ref, gfq_ref, gfk_ref, gsq_ref, gsk_ref, fb_ref,
                    gcq_ref, wqu_ref, gckv_ref, wkvu_ref, gmq_n_ref, gmq_r_ref, gmk_n_ref, gmk_r_ref,
                    cos_ref, sin_ref, tri_ref, eq_ref, ek_ref, qconst_ref, kconst_ref,
                    qf_ref, kf_ref, vf_ref, qm_ref, km_ref, vm_ref, sq_ref, sk_ref, sv_ref,
                    carry_ref, *, tm):
    j = pl.program_id(1)
    h = h_ref[...]
    xn = (_rms_rows(h, h.shape[-1]) * g1_ref[...]).astype(BF16)

    def proj(c0, width):
        return jnp.dot(xn, wa_ref[:, c0:c0 + width], preferred_element_type=F32)

    yq = proj(_A_FQ, 512)
    yk = proj(_A_FK, 512)
    for t in range(4):
        sl = slice(t * LANES, (t + 1) * LANES)
        qf_ref[:, 2 * t * LANES:(2 * t + 1) * LANES] = (_headnorm64_tile(yq[:, sl]) * gfq_ref[:, sl]).astype(BF16)
        kf_ref[:, 2 * t * LANES:(2 * t + 1) * LANES] = (_headnorm64_tile(yk[:, sl]) * gfk_ref[:, sl]).astype(BF16)
    vf_ref[...] = proj(_A_FV, 512).astype(BF16)

    @pl.when(j == 0)
    def _():
        carry_ref[...] = jnp.zeros_like(carry_ref)

    z = proj(_A_FF, LANES) + fb_ref[...]
    ls = jnp.minimum(z, 0.0) - jnp.log1p(jnp.exp(-jnp.abs(z)))
    row = j * tm + lax.broadcasted_iota(jnp.int32, ls.shape, 0)
    ls = jnp.where((_lane_iota(ls.shape) < N_HEADS) & (row >= PAD), ls * LOG2E, 0.0)
    tri = tri_ref[...]
    c = carry_ref[0:1, :]
    rem = ls
    for _ in range(DEC_PARTS):
        part = rem.astype(BF16)
        c = c + jnp.dot(tri, part, preferred_element_type=F32)
        rem = rem - part.astype(F32)
    carry_ref[0:1, :] = c[tm - 1:tm, :]
    qd = qconst_ref[...]
    kd = kconst_ref[...]
    rem = c
    for i in range(DEC_PARTS):
        part = rem.astype(BF16)
        qd = qd + jnp.dot(part, eq_ref[i], preferred_element_type=F32)
        kd = kd + jnp.dot(part, ek_ref[i], preferred_element_type=F32)
        rem = rem - part.astype(F32)
    qd = qd.astype(BF16)
    kd = kd.astype(BF16)
    for t in range(4):
        qf_ref[:, (2 * t + 1) * LANES:(2 * t + 2) * LANES] = qd
        kf_ref[:, (2 * t + 1) * LANES:(2 * t + 2) * LANES] = kd

    ysq = proj(_A_SQ, 512)
    for t in range(4):
        sl = slice(t * LANES, (t + 1) * LANES)
        sq_ref[:, sl] = (_headnorm64_tile(ysq[:, sl]) * gsq_ref[:, sl]).astype(BF16)
    sk_ref[...] = (_headnorm64_tile(proj(_A_SK, LANES)) * gsk_ref[...]).astype(BF16)
    sv_ref[...] = proj(_A_SV, LANES).astype(BF16)

    cos = cos_ref[...]
    sin = sin_ref[...]

    def rope(x):
        return x * cos + _rotate_half(x) * sin

    cq = (_rms_rows(proj(_A_CQ, MLA_Q_RANK), MLA_Q_RANK) * gcq_ref[...]).astype(BF16)
    yq = jnp.dot(cq, wqu_ref[...], preferred_element_type=F32)
    lane = _lane_iota((tm, LANES))
    quad_id = lane // ROPE_DIM
    ss = []
    for t in range(4):
        lo, hi = _half_sums(jnp.square(yq[:, t * LANES:(t + 1) * LANES]))
        ss += [lo, hi]
    for u in range(2):
        y2 = jnp.square(yq[:, 512 + u * LANES:512 + (u + 1) * LANES])
        for g in range(4):
            ss[4 * u + g] = ss[4 * u + g] + jnp.sum(jnp.where(quad_id == g, y2, 0.0), axis=-1, keepdims=True)
    width = HEAD_DIM + ROPE_DIM
    rq = [lax.rsqrt(s * (1.0 / width) + EPS) for s in ss]
    q_rope = []
    for u in range(2):
        sl = slice(512 + u * LANES, 512 + (u + 1) * LANES)
        x = yq[:, sl] * _quad_select(lane, rq[4 * u:4 * u + 4]) * gmq_r_ref[...]
        q_rope.append(rope(x).astype(BF16))
    for t in range(4):
        sl = slice(t * LANES, (t + 1) * LANES)
        r = jnp.where(lane < HEAD_DIM, rq[2 * t], rq[2 * t + 1])
        qm_ref[:, 2 * t * LANES:(2 * t + 1) * LANES] = (yq[:, sl] * r * gmq_n_ref[:, sl]).astype(BF16)
        qm_ref[:, (2 * t + 1) * LANES:(2 * t + 2) * LANES] = q_rope[t // 2]

    ckv = (_rms_rows(proj(_A_CKV, MLA_KV_RANK), MLA_KV_RANK) * gckv_ref[...]).astype(BF16)
    ykv = jnp.dot(ckv, wkvu_ref[...], preferred_element_type=F32)
    kr4 = proj(_A_KR, LANES)
    ss_rope = jnp.sum(kr4 * kr4, axis=-1, keepdims=True) * 0.25
    rk = []
    for t in range(4):
        lo, hi = _half_sums(jnp.square(ykv[:, t * LANES:(t + 1) * LANES]))
        rk += [lax.rsqrt((lo + ss_rope) * (1.0 / width) + EPS), lax.rsqrt((hi + ss_rope) * (1.0 / width) + EPS)]
    kr_base = rope(kr4 * gmk_r_ref[...])
    k_rope = [(kr_base * _quad_select(lane, rk[4 * u:4 * u + 4])).astype(BF16) for u in range(2)]
    for t in range(4):
        sl = slice(t * LANES, (t + 1) * LANES)
        r = jnp.where(lane < HEAD_DIM, rk[2 * t], rk[2 * t + 1])
        km_ref[:, 2 * t * LANES:(2 * t + 1) * LANES] = (ykv[:, sl] * r * gmk_n_ref[:, sl]).astype(BF16)
        km_ref[:, (2 * t + 1) * LANES:(2 * t + 2) * LANES] = k_rope[t // 2]
    vm_ref[...] = ykv[:, 512:1024].astype(BF16)


def _in_proj(h, p, consts, tm):
    b, lp, d = h.shape
    nt = lp // tm
    row_blk = lambda w: pl.BlockSpec((None, tm, w), lambda bi, j: (bi, j, 0))
    full = lambda a: pl.BlockSpec(a.shape, lambda bi, j: (0,) * a.ndim)
    tab = pl.BlockSpec((tm, LANES), lambda bi, j: (j, 0))
    ins = [h, p['g1'], p['wa'], p['gfq'], p['gfk'], p['gsq'], p['gsk'], p['fb'],
           p['gcq'], p['wqu'], p['gckv'], p['wkvu'], p['gmq_n'], p['gmq_r'], p['gmk_n'], p['gmk_r']]
    in_specs = [row_blk(d)] + [full(a) for a in ins[1:]]
    ins += [consts['cos'], consts['sin'], consts['tri'], consts['eq'], consts['ek'], consts['qconst'], consts['kconst']]
    in_specs += [tab, tab] + [full(consts[k]) for k in ('tri', 'eq', 'ek', 'qconst', 'kconst')]
    widths = [1024, 1024, 512, 1024, 1024, 512, 512, 128, 128]
    out_shape = [jax.ShapeDtypeStruct((b, lp, w), BF16) for w in widths]
    out_specs = [row_blk(w) for w in widths]
    return pl.pallas_call(
        functools.partial(_in_proj_kernel, tm=tm),
        grid=(b, nt),
        in_specs=in_specs,
        out_specs=out_specs,
        out_shape=out_shape,
        scratch_shapes=[pltpu.VMEM((8, LANES), F32)],
        compiler_params=pltpu.CompilerParams(dimension_semantics=("arbitrary", "arbitrary"),
                                             vmem_limit_bytes=VMEM_LIMIT),
        name="in_proj",
    )(*ins)


def _causal_attn_kernel(mask_ref, q_ref, k_ref, vt_ref, o_ref, m_ref, acc_ref, *, tq, n_blocks):
    masks = mask_ref[...]
    nt = (((1,), (1,)), ((), ()))

    def init(n):
        m_ref[:, :, 0:n] = jnp.full((2, 1, n), NEG, F32)
        acc_ref[:, :, 0:n] = jnp.zeros((2, VT_ROWS, n), F32)

    def chunk(qs, units, k0, tk, mask_fn):
        k = k_ref[pl.ds(k0, tk), :]
        sts = [lax.dot_general(k, qs[a][c0:c0 + nc, :], nt, preferred_element_type=F32) for a, c0, nc in units]
        for (a, c0, nc), st in zip(units, sts):
            if mask_fn is not None:
                st = jnp.where(mask_fn(st.shape, c0), st, NEG)
            cols = slice(c0, c0 + nc)
            m_prev = m_ref[a, :, cols]
            m_new = jnp.maximum(m_prev, jnp.max(st, axis=0, keepdims=True))
            alpha = jnp.exp2(m_prev - m_new)
            p = jnp.exp2(st - m_new).astype(BF16)
            vt = vt_ref[a * VT_ROWS:(a + 1) * VT_ROWS, pl.ds(k0, tk)]
            acc_ref[a, :, cols] = alpha * acc_ref[a, :, cols] + jnp.dot(vt, p, preferred_element_type=F32)
            m_ref[a, :, cols] = m_new

    def finish(r0, n):
        ot = jnp.concatenate([acc_ref[a, 0:HEAD_DIM, 0:n] * (1.0 / acc_ref[a, HEAD_DIM:HEAD_DIM + 1, 0:n])
                              for a in range(2)], axis=0)
        o_ref[pl.ds(r0, n), :] = ot.T.astype(o_ref.dtype)

    def causal(offset):
        def fn(shape, c0):
            return (lax.broadcasted_iota(jnp.int32, shape, 0) - lax.broadcasted_iota(jnp.int32, shape, 1)) <= offset + c0
        return fn

    def not_pad(shape, c0):
        return lax.broadcasted_iota(jnp.int32, shape, 0) >= PAD

    def head_queries(q):
        return [q * masks[a:a + 1, :] for a in range(2)]

    init(FRONT)
    chunk(head_queries(q_ref[0:FRONT, :]), [(0, 0, FRONT), (1, 0, FRONT)], 0, FRONT,
          lambda shape, c0: causal(0)(shape, c0) & not_pad(shape, c0))
    finish(0, FRONT)

    n_sub = tq // QUERY_SUB
    units = [(a, s * QUERY_SUB, QUERY_SUB) for a in range(2) for s in range(n_sub)]

    def q_block(i, carry):
        r0 = pl.multiple_of(FRONT + i * tq, LANES)
        qs = head_queries(q_ref[pl.ds(r0, tq), :])
        init(tq)
        chunk(qs, units, 0, FRONT, not_pad)

        def kv_block(jj, c):
            chunk(qs, units, pl.multiple_of(FRONT + jj * tq, LANES), tq, None)
            return c

        lax.fori_loop(0, i, kv_block, 0)
        for s in range(n_sub):
            chunk(qs, [u for u in units if u[1] == s * QUERY_SUB], r0, (s + 1) * QUERY_SUB, causal(0))
        finish(r0, tq)
        return carry

    lax.fori_loop(0, n_blocks, q_block, 0)


def _causal_attn(q, k, v, masks, tq):
    b, lp, _ = v.shape
    n_pairs = N_HEADS // 2
    n_blocks = (lp - FRONT) // tq
    assert tq % QUERY_SUB == 0
    vt = jnp.swapaxes(v, 1, 2).reshape(b, N_HEADS, HEAD_DIM, lp)
    vt = jnp.concatenate([vt, jnp.ones((b, N_HEADS, ONES_ROWS, lp), v.dtype)], axis=2).reshape(b, N_HEADS * VT_ROWS, lp)

    def qk_spec():
        return pl.BlockSpec((None, lp, 2 * LANES), lambda bi, p: (bi, 0, p))

    return pl.pallas_call(
        functools.partial(_causal_attn_kernel, tq=tq, n_blocks=n_blocks),
        grid=(b, n_pairs),
        in_specs=[pl.BlockSpec((None, 2, 2 * LANES), lambda bi, p: (p, 0, 0)),
                  qk_spec(), qk_spec(),
                  pl.BlockSpec((None, 2 * VT_ROWS, lp), lambda bi, p: (bi, p, 0))],
        out_specs=pl.BlockSpec((None, lp, LANES), lambda bi, p: (bi, 0, p)),
        out_shape=jax.ShapeDtypeStruct((b, lp, BRANCH_WIDTH), BF16),
        scratch_shapes=[pltpu.VMEM((2, 1, tq), F32), pltpu.VMEM((2, VT_ROWS, tq), F32)],
        compiler_params=pltpu.CompilerParams(dimension_semantics=("arbitrary", "arbitrary"),
                                             vmem_limit_bytes=VMEM_LIMIT),
        name="causal_attn",
    )(masks, q, k, vt)


def _swa_kernel(sinks_ref, q_ref, k_ref, v_ref, o_ref):
    i = pl.program_id(1)
    r0 = i * FRONT
    start = pl.multiple_of(jnp.maximum(r0 - WINDOW, 0), LANES)
    kk = jnp.concatenate([k_ref[0:FRONT, :], k_ref[pl.ds(start, 2 * WINDOW), :]], axis=0)
    vv = jnp.concatenate([v_ref[0:FRONT, :], v_ref[pl.ds(start, 2 * WINDOW), :]], axis=0)
    shape = (FRONT, FRONT + 2 * WINDOW)
    row = lax.broadcasted_iota(jnp.int32, shape, 0)
    col = lax.broadcasted_iota(jnp.int32, shape, 1)
    qpos = r0 + row
    in_front = col < FRONT
    kpos = jnp.where(in_front, col, start + col - FRONT)
    lowest = jnp.where(in_front, PAD, jnp.maximum(qpos - (WINDOW - 1), FRONT))
    valid = (kpos >= lowest) & (kpos <= qpos)
    dist = (qpos - kpos).astype(F32)
    lane = _lane_iota((FRONT, LANES))
    lane_q = _lane_iota((1, LANES))
    nt = (((1,), (1,)), ((), ()))
    for t in range(4):
        qt = q_ref[:, t * LANES:(t + 1) * LANES]
        outs = []
        for a in range(2):
            head = t + 4 * a
            slope = 2.0 ** (-8.0 * (head + 1) / N_HEADS)
            keep = ((lane_q >= HEAD_DIM) if a else (lane_q < HEAD_DIM)).astype(BF16)
            s = lax.dot_general(qt * keep, kk, nt, preferred_element_type=F32) - slope * dist
            s = jnp.where(valid, s, NEG)
            sink = sinks_ref[head]
            m = jnp.maximum(jnp.max(s, axis=-1, keepdims=True), sink)
            p = jnp.exp(s - m)
            den = jnp.sum(p, axis=-1, keepdims=True) + jnp.exp(sink - m)
            outs.append(jnp.dot(p.astype(BF16), vv, preferred_element_type=F32) * (1.0 / den))
        o_ref[:, t * LANES:(t + 1) * LANES] = jnp.where(lane < HEAD_DIM, outs[0], outs[1]).astype(o_ref.dtype)


def _swa_attn(q, k, v, sinks):
    b, lp, _ = q.shape
    return pl.pallas_call(
        _swa_kernel,
        grid=(b, lp // FRONT),
        in_specs=[pl.BlockSpec(memory_space=pltpu.SMEM),
                  pl.BlockSpec((None, FRONT, BRANCH_WIDTH), lambda bi, i: (bi, i, 0)),
                  pl.BlockSpec((None, lp, LANES), lambda bi, i: (bi, 0, 0)),
                  pl.BlockSpec((None, lp, LANES), lambda bi, i: (bi, 0, 0))],
        out_specs=pl.BlockSpec((None, FRONT, BRANCH_WIDTH), lambda bi, i: (bi, i, 0)),
        out_shape=jax.ShapeDtypeStruct((b, lp, BRANCH_WIDTH), BF16),
        compiler_params=pltpu.CompilerParams(dimension_semantics=("arbitrary", "arbitrary"),
                                             vmem_limit_bytes=VMEM_LIMIT),
        name="swa_attn",
    )(sinks, q, k, v)


def _merge_kernel(h_ref, oa_ref, ob_ref, oc_ref, g1_ref, wg_ref, wb_ref, wo_ref, out_ref):
    h = h_ref[...]
    d = h.shape[-1]
    xn = (_rms_rows(h, d) * g1_ref[...]).astype(BF16)
    merged = None
    for n, o_ref in enumerate((oa_ref, ob_ref, oc_ref)):
        gate = jax.nn.sigmoid(jnp.dot(xn, wg_ref[:, n * d:(n + 1) * d], preferred_element_type=F32))
        y = jnp.dot(o_ref[...], wb_ref[n], preferred_element_type=F32)
        merged = gate * y if merged is None else merged + gate * y
    out_ref[...] = h + jnp.dot(merged.astype(BF16), wo_ref[...], preferred_element_type=F32)


def _merge(h, oa, ob, oc, p, tm):
    b, lp, d = h.shape
    row_blk = lambda w: pl.BlockSpec((None, tm, w), lambda bi, j: (bi, j, 0))
    full = lambda a: pl.BlockSpec(a.shape, lambda bi, j: (0,) * a.ndim)
    ws = [p['g1'], p['wg'], p['wb'], p['wo']]
    return pl.pallas_call(
        _merge_kernel,
        grid=(b, lp // tm),
        in_specs=[row_blk(d), row_blk(BRANCH_WIDTH), row_blk(BRANCH_WIDTH), row_blk(BRANCH_WIDTH)] + [full(a) for a in ws],
        out_specs=row_blk(d),
        out_shape=jax.ShapeDtypeStruct(h.shape, F32),
        compiler_params=pltpu.CompilerParams(dimension_semantics=("arbitrary", "arbitrary"),
                                             vmem_limit_bytes=VMEM_LIMIT),
        name="merge",
    )(h, oa, ob, oc, *ws)


_FF_CHUNK = 256
_HALO = 8


def _ffn_kernel(h_ref, g2_ref, wup_ref, cw_ref, cb_ref, wdn_ref, out_ref, carry_ref, ubuf_ref, act_ref, *, tm, d_ff):
    j = pl.program_id(1)
    h = h_ref[...]
    xn = _rms_rows(h, h.shape[-1]) * g2_ref[...]
    row = j * tm + lax.broadcasted_iota(jnp.int32, (tm, 1), 0)
    xn = jnp.where(row >= PAD, xn, 0.0).astype(BF16)

    @pl.when(j == 0)
    def _():
        carry_ref[...] = jnp.zeros_like(carry_ref)

    for c in range(d_ff // _FF_CHUNK):
        acts = []
        for half in range(2):
            c0 = half * d_ff + c * _FF_CHUNK
            cols = slice(c0, c0 + _FF_CHUNK)
            u = jnp.dot(xn, wup_ref[:, cols], preferred_element_type=F32)
            ubuf_ref[half, 0:_HALO, :] = carry_ref[:, cols]
            ubuf_ref[half, _HALO:_HALO + tm, :] = u
            carry_ref[:, cols] = u[tm - _HALO:tm, :]
            u1 = ubuf_ref[half, _HALO - 1:_HALO - 1 + tm, :]
            u2 = ubuf_ref[half, _HALO - 2:_HALO - 2 + tm, :]
            acts.append(cb_ref[:, cols] + cw_ref[2:3, cols] * u + cw_ref[1:2, cols] * u1 + cw_ref[0:1, cols] * u2)
        gate, val = acts
        act_ref[:, c * _FF_CHUNK:(c + 1) * _FF_CHUNK] = (gate * jax.nn.sigmoid(gate) * val).astype(BF16)
    out_ref[...] = h + jnp.dot(act_ref[...], wdn_ref[...], preferred_element_type=F32)


def _ffn(h, p, tm):
    b, lp, d = h.shape
    d_ff = p['wdn'].shape[0]
    row_blk = pl.BlockSpec((None, tm, d), lambda bi, j: (bi, j, 0))
    full = lambda a: pl.BlockSpec(a.shape, lambda bi, j: (0,) * a.ndim, pipeline_mode=pl.Buffered(1))
    ws = [p['g2'], p['wup'], p['cw'], p['cb'], p['wdn']]
    return pl.pallas_call(
        functools.partial(_ffn_kernel, tm=tm, d_ff=d_ff),
        grid=(b, lp // tm),
        in_specs=[row_blk] + [full(a) for a in ws],
        out_specs=row_blk,
        out_shape=jax.ShapeDtypeStruct(h.shape, F32),
        scratch_shapes=[pltpu.VMEM((_HALO, 2 * d_ff), F32),
                        pltpu.VMEM((2, _HALO + tm, _FF_CHUNK), F32),
                        pltpu.VMEM((tm, d_ff), BF16)],
        compiler_params=pltpu.CompilerParams(dimension_semantics=("arbitrary", "arbitrary"),
                                             vmem_limit_bytes=VMEM_LIMIT),
        name="conv_ffn",
    )(h, *ws)


def _constants(lp, tm):
    half = ROPE_DIM // 2
    pos = (np.arange(lp) - PAD).astype(np.float32)
    freqs = (ROPE_THETA ** (-np.arange(half, dtype=np.float32) / half)).astype(np.float32)
    lane = np.arange(LANES)
    ang = jnp.asarray(pos)[:, None] * jnp.asarray(freqs[lane % half])[None, :]
    sign = np.where((lane % ROPE_DIM) < half, -1.0, 1.0).astype(np.float32)
    tri = np.tril(np.ones((tm, tm), np.float32))
    eq = np.zeros((DEC_PARTS, LANES, LANES), np.float32)
    ek = np.zeros((DEC_PARTS, LANES, LANES), np.float32)
    qconst = np.zeros((1, LANES), np.float32)
    kconst = np.zeros((1, LANES), np.float32)
    for hd in range(N_HEADS):
        base = 2 * DEC_PARTS * hd
        for i in range(DEC_PARTS):
            eq[i, hd, base + i] = 1.0
            ek[i, hd, base + DEC_PARTS + i] = -1.0
            qconst[0, base + DEC_PARTS + i] = 1.0
            kconst[0, base + i] = 1.0
    fox_masks = np.zeros((N_HEADS // 2, 2, 2 * LANES), np.float32)
    mla_masks = np.zeros((N_HEADS // 2, 2, 2 * LANES), np.float32)
    for p in range(N_HEADS // 2):
        for a in range(2):
            hd = 2 * p + a
            fox_masks[p, a, a * HEAD_DIM:(a + 1) * HEAD_DIM] = 1.0
            fox_masks[p, a, LANES + 2 * DEC_PARTS * hd:LANES + 2 * DEC_PARTS * (hd + 1)] = 1.0
            mla_masks[p, a, a * HEAD_DIM:(a + 1) * HEAD_DIM] = 1.0
            g = hd % 4
            mla_masks[p, a, LANES + g * ROPE_DIM:LANES + (g + 1) * ROPE_DIM] = 1.0
    return {
        'cos': jnp.cos(ang), 'sin': jnp.sin(ang) * jnp.asarray(sign)[None, :],
        'tri': jnp.asarray(tri, BF16), 'eq': jnp.asarray(eq, BF16), 'ek': jnp.asarray(ek, BF16),
        'qconst': jnp.asarray(qconst), 'kconst': jnp.asarray(kconst),
        'fox_masks': jnp.asarray(fox_masks, BF16), 'mla_masks': jnp.asarray(mla_masks, BF16),
    }


def _swa_head_perm():
    cols = []
    for t in range(4):
        for hd in (t, 4 + t):
            cols += list(range(hd * HEAD_DIM, (hd + 1) * HEAD_DIM))
    return np.asarray(cols)


def _layer_params(l, norm1_g, w_in, fox_forget_b, fox_q_g, fox_k_g, mla_q_a_g, mla_w_q_up, mla_kv_a_g,
                  mla_w_kv_up, mla_q_g, mla_k_g, swa_q_g, swa_k_g, swa_sinks, w_branch, w_o,
                  norm2_g, ffn_w_up, ffn_conv_w, ffn_conv_b, ffn_w_down):
    d = w_in.shape[1]
    w = w_in[l]
    o_fq, o_fk, o_fv, o_ff = 0, 512, 1024, 1536
    o_cq = o_ff + N_HEADS
    o_ckv = o_cq + MLA_Q_RANK
    o_kr = o_ckv + MLA_KV_RANK
    o_sq = o_kr + ROPE_DIM
    o_sk = o_sq + 512
    o_sv = o_sk + SWA_KV_HEADS * HEAD_DIM
    o_g = o_sv + SWA_KV_HEADS * HEAD_DIM
    perm = _swa_head_perm()
    wa = jnp.concatenate([
        w[:, o_fq:o_fq + 512], w[:, o_fk:o_fk + 512], w[:, o_fv:o_fv + 512],
        w[:, o_sq:o_sq + 512][:, perm], w[:, o_sk:o_sk + 128], w[:, o_sv:o_sv + 128],
        w[:, o_cq:o_cq + MLA_Q_RANK], w[:, o_ckv:o_ckv + MLA_KV_RANK],
        jnp.tile(w[:, o_kr:o_kr + ROPE_DIM], (1, LANES // ROPE_DIM)),
        w[:, o_ff:o_ff + N_HEADS], jnp.zeros((d, LANES - N_HEADS), w.dtype)], axis=1).astype(BF16)
    tile_row = lambda g, n: jnp.tile(g, n)[None, :].astype(F32)
    width = HEAD_DIM + ROPE_DIM
    qcols = np.concatenate([np.concatenate([np.arange(hd * width, hd * width + HEAD_DIM) for hd in range(N_HEADS)]),
                            np.concatenate([np.arange(hd * width + HEAD_DIM, (hd + 1) * width) for hd in range(N_HEADS)])])
    kvcols = np.concatenate([np.concatenate([np.arange(hd * 2 * HEAD_DIM, hd * 2 * HEAD_DIM + HEAD_DIM) for hd in range(N_HEADS)]),
                             np.concatenate([np.arange(hd * 2 * HEAD_DIM + HEAD_DIM, (hd + 1) * 2 * HEAD_DIM) for hd in range(N_HEADS)])])
    fb = jnp.concatenate([fox_forget_b[l], jnp.zeros((LANES - N_HEADS,), F32)])[None, :]
    wb = w_branch[l]
    wb = jnp.stack([wb[0], wb[1], wb[2][perm, :]]).astype(BF16)
    return {
        'g1': norm1_g[l][None, :], 'wa': wa,
        'gfq': tile_row(fox_q_g[l], N_HEADS) * (LOG2E * HEAD_DIM ** -0.5), 'gfk': tile_row(fox_k_g[l], N_HEADS),
        'gsq': tile_row(swa_q_g[l], N_HEADS) * (HEAD_DIM ** -0.5), 'gsk': tile_row(swa_k_g[l], SWA_KV_HEADS),
        'fb': fb,
        'gcq': mla_q_a_g[l][None, :], 'wqu': mla_w_q_up[l][:, qcols].astype(BF16),
        'gckv': mla_kv_a_g[l][None, :], 'wkvu': mla_w_kv_up[l][:, kvcols].astype(BF16),
        'gmq_n': tile_row(mla_q_g[l][:HEAD_DIM], N_HEADS) * (LOG2E * width ** -0.5),
        'gmq_r': tile_row(mla_q_g[l][HEAD_DIM:], LANES // ROPE_DIM) * (LOG2E * width ** -0.5),
        'gmk_n': tile_row(mla_k_g[l][:HEAD_DIM], N_HEADS),
        'gmk_r': tile_row(mla_k_g[l][HEAD_DIM:], LANES // ROPE_DIM),
        'sinks': swa_sinks[l].astype(F32),
        'wg': w[:, o_g:].astype(BF16), 'wb': wb, 'wo': w_o[l].astype(BF16),
        'g2': norm2_g[l][None, :], 'wup': ffn_w_up[l].astype(BF16), 'cw': ffn_conv_w[l], 'cb': ffn_conv_b[l][None, :],
        'wdn': ffn_w_down[l].astype(BF16),
    }


def kernel(x, meta_tokens, norm1_g, w_in, fox_forget_b, fox_q_g, fox_k_g, mla_q_a_g, mla_w_q_up, mla_kv_a_g, mla_w_kv_up, mla_q_g, mla_k_g, swa_q_g, swa_k_g, swa_sinks, w_branch, w_o, norm2_g, ffn_w_up, ffn_conv_w, ffn_conv_b, ffn_w_down):
    b, seq, d = x.shape
    assert seq % LANES == 0 and meta_tokens.shape == (N_META, d)
    lp = FRONT + seq
    tm = _row_tile(lp)
    tq = _row_tile(seq)
    consts = _constants(lp, tm)
    front = jnp.concatenate([jnp.zeros((PAD, d), x.dtype), meta_tokens.astype(x.dtype)], axis=0)
    h = jnp.concatenate([jnp.broadcast_to(front[None], (b, FRONT, d)), x], axis=1)
    weights = (norm1_g, w_in, fox_forget_b, fox_q_g, fox_k_g, mla_q_a_g, mla_w_q_up, mla_kv_a_g, mla_w_kv_up,
               mla_q_g, mla_k_g, swa_q_g, swa_k_g, swa_sinks, w_branch, w_o, norm2_g, ffn_w_up, ffn_conv_w,
               ffn_conv_b, ffn_w_down)
    for l in range(w_in.shape[0]):
        p = _layer_params(l, *weights)
        qf, kf, vf, qm, km, vm, sq, sk, sv = _in_proj(h, p, consts, tm)
        out_a = _causal_attn(qf, kf, vf, consts['fox_masks'], tq)
        out_b = _causal_attn(qm, km, vm, consts['mla_masks'], tq)
        out_c = _swa_attn(sq, sk, sv, p['sinks'])
        h = _merge(h, out_a, out_b, out_c, p, tm)
        h = _ffn(h, p, tm)
    return h[:, FRONT:]
```

```python
import functools
import math

import numpy as np
import jax
import jax.numpy as jnp
from jax import lax
from jax.experimental import pallas as pl
from jax.experimental.pallas import tpu as pltpu

F32 = jnp.float32
BF16 = jnp.bfloat16

N_META = 16
EPS = 1e-6
N_HEADS = 8
HEAD_DIM = 64
ROPE_DIM = 32
MLA_Q_RANK = 256
MLA_KV_RANK = 128
SWA_KV_HEADS = 2
WINDOW = 128
ROPE_THETA = 10000.0
BRANCH_WIDTH = N_HEADS * HEAD_DIM
N_BRANCH = 3
CONV_WIDTH = 3

LANES = 128
FRONT = 128
PAD = FRONT - N_META
NEG = -1e30
DEC_PARTS = 3
LOG2E = math.log2(math.e)
ONES_ROWS = 16
VT_ROWS = HEAD_DIM + ONES_ROWS
QUERY_SUB = 256
PREFETCH = 2
SWA_QUERY_BLOCK = 256
VMEM_LIMIT = 56 * 1024 * 1024

_A_FQ, _A_FK, _A_SQ, _A_SK, _A_CQ, _A_CKV, _A_KR, _A_FF = 0, 512, 1024, 1536, 1664, 1920, 2048, 2176
_NT = (((1,), (1,)), ((), ()))


def _row_tile(lp):
    best = 128
    for t in range(128, 513, 128):
        if lp % t == 0:
            best = t
    return best


def _lane_iota(shape):
    return lax.broadcasted_iota(jnp.int32, shape, len(shape) - 1)


def _rms_rows(y, width):
    ss = jnp.sum(y * y, axis=-1, keepdims=True)
    return y * lax.rsqrt(ss * (1.0 / width) + EPS)


def _half_sums(y2):
    lane = _lane_iota(y2.shape)
    lo = jnp.sum(jnp.where(lane < HEAD_DIM, y2, 0.0), axis=-1, keepdims=True)
    hi = jnp.sum(jnp.where(lane >= HEAD_DIM, y2, 0.0), axis=-1, keepdims=True)
    return lo, hi


def _headnorm64_tile(yt):
    lo, hi = _half_sums(yt * yt)
    lane = _lane_iota(yt.shape)
    r = jnp.where(lane < HEAD_DIM, lax.rsqrt(lo * (1.0 / HEAD_DIM) + EPS), lax.rsqrt(hi * (1.0 / HEAD_DIM) + EPS))
    return yt * r


def _rotate_half(x):
    lane = _lane_iota(x.shape)
    half = ROPE_DIM // 2
    return jnp.where((lane % ROPE_DIM) < half, pltpu.roll(x, LANES - half, 1), pltpu.roll(x, half, 1))


def _quad_select(lane, vals):
    g = lane // ROPE_DIM
    out = jnp.where(g == 0, vals[0], vals[1])
    out = jnp.where(g == 2, vals[2], out)
    return jnp.where(g == 3, vals[3], out)


def _in_proj_kernel(h_ref, g1_ref, wa_ref, gfq_ref, gfk_ref, gsq_ref, gsk_ref, fb_ref,
                    gcq_ref, wqu_ref, gckv_ref, wkvu_ref, gmq_n_ref, gmq_r_ref, gmk_n_ref, gmk_r_ref,
                    wvtf_ref, wvtm_ref, wvts_ref,
                    cos_ref, sin_ref, tri_ref, eq_ref, ek_ref, qconst_ref, kconst_ref,
                    qf_ref, kf_ref, vtf_ref, qm_ref, km_ref, vtm_ref, sq_ref, sk_ref, vts_ref,
                    carry_ref, *, tm):
    j = pl.program_id(1)
    h = h_ref[...]
    xn = (_rms_rows(h, h.shape[-1]) * g1_ref[...]).astype(BF16)

    def proj(c0, width):
        return jnp.dot(xn, wa_ref[:, c0:c0 + width], preferred_element_type=F32)

    def store_vt(vt_ref, wt_ref, x):
        yt = lax.dot_general(wt_ref[...], x, _NT, preferred_element_type=F32)
        ones = jnp.ones((ONES_ROWS, tm), BF16)
        for hd in range(yt.shape[0] // HEAD_DIM):
            vt_ref[hd * VT_ROWS:hd * VT_ROWS + HEAD_DIM, :] = yt[hd * HEAD_DIM:(hd + 1) * HEAD_DIM, :].astype(BF16)
            vt_ref[hd * VT_ROWS + HEAD_DIM:(hd + 1) * VT_ROWS, :] = ones

    yq = proj(_A_FQ, 512)
    yk = proj(_A_FK, 512)
    for t in range(4):
        sl = slice(t * LANES, (t + 1) * LANES)
        qf_ref[:, 2 * t * LANES:(2 * t + 1) * LANES] = (_headnorm64_tile(yq[:, sl]) * gfq_ref[:, sl]).astype(BF16)
        kf_ref[:, 2 * t * LANES:(2 * t + 1) * LANES] = (_headnorm64_tile(yk[:, sl]) * gfk_ref[:, sl]).astype(BF16)
    store_vt(vtf_ref, wvtf_ref, xn)

    @pl.when(j == 0)
    def _():
        carry_ref[...] = jnp.zeros_like(carry_ref)

    z = proj(_A_FF, LANES) + fb_ref[...]
    ls = jnp.minimum(z, 0.0) - jnp.log1p(jnp.exp(-jnp.abs(z)))
    row = j * tm + lax.broadcasted_iota(jnp.int32, ls.shape, 0)
    ls = jnp.where((_lane_iota(ls.shape) < N_HEADS) & (row >= PAD), ls * LOG2E, 0.0)
    tri = tri_ref[...]
    c = carry_ref[0:1, :]
    rem = ls
    for _ in range(DEC_PARTS):
        part = rem.astype(BF16)
        c = c + jnp.dot(tri, part, preferred_element_type=F32)
        rem = rem - part.astype(F32)
    carry_ref[0:1, :] = c[tm - 1:tm, :]
    qd = qconst_ref[...]
    kd = kconst_ref[...]
    rem = c
    for i in range(DEC_PARTS):
        part = rem.astype(BF16)
        qd = qd + jnp.dot(part, eq_ref[i], preferred_element_type=F32)
        kd = kd + jnp.dot(part, ek_ref[i], preferred_element_type=F32)
        rem = rem - part.astype(F32)
    qd = qd.astype(BF16)
    kd = kd.astype(BF16)
    for t in range(4):
        qf_ref[:, (2 * t + 1) * LANES:(2 * t + 2) * LANES] = qd
        kf_ref[:, (2 * t + 1) * LANES:(2 * t + 2) * LANES] = kd

    ysq = proj(_A_SQ, 512)
    for t in range(4):
        sl = slice(t * LANES, (t + 1) * LANES)
        sq_ref[:, sl] = (_headnorm64_tile(ysq[:, sl]) * gsq_ref[:, sl]).astype(BF16)
    sk_ref[...] = (_headnorm64_tile(proj(_A_SK, LANES)) * gsk_ref[...]).astype(BF16)
    store_vt(vts_ref, wvts_ref, xn)

    cos = cos_ref[...]
    sin = sin_ref[...]

    def rope(x):
        return x * cos + _rotate_half(x) * sin

    cq = (_rms_rows(proj(_A_CQ, MLA_Q_RANK), MLA_Q_RANK) * gcq_ref[...]).astype(BF16)
    yq = jnp.dot(cq, wqu_ref[...], preferred_element_type=F32)
    lane = _lane_iota((tm, LANES))
    quad_id = lane // ROPE_DIM
    ss = []
    for t in range(4):
        lo, hi = _half_sums(jnp.square(yq[:, t * LANES:(t + 1) * LANES]))
        ss += [lo, hi]
    for u in range(2):
        y2 = jnp.square(yq[:, 512 + u * LANES:512 + (u + 1) * LANES])
        for g in range(4):
            ss[4 * u + g] = ss[4 * u + g] + jnp.sum(jnp.where(quad_id == g, y2, 0.0), axis=-1, keepdims=True)
    width = HEAD_DIM + ROPE_DIM
    rq = [lax.rsqrt(s * (1.0 / width) + EPS) for s in ss]
    q_rope = []
    for u in range(2):
        sl = slice(512 + u * LANES, 512 + (u + 1) * LANES)
        x = yq[:, sl] * _quad_select(lane, rq[4 * u:4 * u + 4]) * gmq_r_ref[...]
        q_rope.append(rope(x).astype(BF16))
    for t in range(4):
        sl = slice(t * LANES, (t + 1) * LANES)
        r = jnp.where(lane < HEAD_DIM, rq[2 * t], rq[2 * t + 1])
        qm_ref[:, 2 * t * LANES:(2 * t + 1) * LANES] = (yq[:, sl] * r * gmq_n_ref[:, sl]).astype(BF16)
        qm_ref[:, (2 * t + 1) * LANES:(2 * t + 2) * LANES] = q_rope[t // 2]

    ckv = (_rms_rows(proj(_A_CKV, MLA_KV_RANK), MLA_KV_RANK) * gckv_ref[...]).astype(BF16)
    ykv = jnp.dot(ckv, wkvu_ref[...], preferred_element_type=F32)
    kr4 = proj(_A_KR, LANES)
    ss_rope = jnp.sum(kr4 * kr4, axis=-1, keepdims=True) * 0.25
    rk = []
    for t in range(4):
        lo, hi = _half_sums(jnp.square(ykv[:, t * LANES:(t + 1) * LANES]))
        rk += [lax.rsqrt((lo + ss_rope) * (1.0 / width) + EPS), lax.rsqrt((hi + ss_rope) * (1.0 / width) + EPS)]
    kr_base = rope(kr4 * gmk_r_ref[...])
    k_rope = [(kr_base * _quad_select(lane, rk[4 * u:4 * u + 4])).astype(BF16) for u in range(2)]
    for t in range(4):
        sl = slice(t * LANES, (t + 1) * LANES)
        r = jnp.where(lane < HEAD_DIM, rk[2 * t], rk[2 * t + 1])
        km_ref[:, 2 * t * LANES:(2 * t + 1) * LANES] = (ykv[:, sl] * r * gmk_n_ref[:, sl]).astype(BF16)
        km_ref[:, (2 * t + 1) * LANES:(2 * t + 2) * LANES] = k_rope[t // 2]
    store_vt(vtm_ref, wvtm_ref, ckv)


def _in_proj(h, p, consts, tm):
    b, lp, d = h.shape
    nt = lp // tm
    row_blk = lambda w: pl.BlockSpec((None, tm, w), lambda bi, j: (bi, j, 0))
    full = lambda a: pl.BlockSpec(a.shape, lambda bi, j: (0,) * a.ndim)
    tab = pl.BlockSpec((tm, LANES), lambda bi, j: (j, 0))
    ins = [h, p['g1'], p['wa'], p['gfq'], p['gfk'], p['gsq'], p['gsk'], p['fb'],
           p['gcq'], p['wqu'], p['gckv'], p['wkvu'], p['gmq_n'], p['gmq_r'], p['gmk_n'], p['gmk_r'],
           p['wvtf'], p['wvtm'], p['wvts']]
    in_specs = [row_blk(d)] + [full(a) for a in ins[1:]]
    ins += [consts['cos'], consts['sin'], consts['tri'], consts['eq'], consts['ek'], consts['qconst'], consts['kconst']]
    in_specs += [tab, tab] + [full(consts[k]) for k in ('tri', 'eq', 'ek', 'qconst', 'kconst')]
    outs = [('row', 1024), ('row', 1024), ('vt', N_HEADS), ('row', 1024), ('row', 1024), ('vt', N_HEADS),
            ('row', 512), ('row', 128), ('vt', SWA_KV_HEADS)]
    out_shape = [jax.ShapeDtypeStruct((b, lp, w) if kind == 'row' else (b, w * VT_ROWS, lp), BF16) for kind, w in outs]
    out_specs = [row_blk(w) if kind == 'row' else pl.BlockSpec((None, w * VT_ROWS, tm), lambda bi, j: (bi, 0, j))
                 for kind, w in outs]
    return pl.pallas_call(
        functools.partial(_in_proj_kernel, tm=tm),
        grid=(b, nt),
        in_specs=in_specs,
        out_specs=out_specs,
        out_shape=out_shape,
        scratch_shapes=[pltpu.VMEM((8, LANES), F32)],
        compiler_params=pltpu.CompilerParams(dimension_semantics=("arbitrary", "arbitrary"),
                                             vmem_limit_bytes=VMEM_LIMIT),
        name="in_proj",
    )(*ins)


def _causal_attn_kernel(mask_ref, q_ref, k_ref, vt_ref, o_ref, m_ref, acc_ref, qs_ref, s_ref, *, tq, n_blocks):
    masks = mask_ref[...]

    def init(n):
        m_ref[:, :, 0:n] = jnp.full((2, 1, n), NEG, F32)
        acc_ref[:, :, 0:n] = jnp.zeros((2, VT_ROWS, n), F32)

    def update(unit, st, vt, visible):
        a, c0, nc = unit
        if visible is not None:
            st = jnp.where(visible, st, NEG)
        cols = slice(c0, c0 + nc)
        m_prev = m_ref[a, :, cols]
        m_new = jnp.maximum(m_prev, jnp.max(st, axis=0, keepdims=True))
        alpha = jnp.exp2(m_prev - m_new)
        p = jnp.exp2(st - m_new).astype(BF16)
        acc_ref[a, :, cols] = alpha * acc_ref[a, :, cols] + jnp.dot(vt, p, preferred_element_type=F32)
        m_ref[a, :, cols] = m_new

    def finish(r0, n):
        ot = jnp.concatenate([acc_ref[a, 0:HEAD_DIM, 0:n] * (1.0 / acc_ref[a, HEAD_DIM:HEAD_DIM + 1, 0:n])
                              for a in range(2)], axis=0)
        o_ref[pl.ds(r0, n), :] = ot.T.astype(o_ref.dtype)

    def set_queries(r0, n):
        q = q_ref[pl.ds(r0, n), :]
        for a in range(2):
            qs_ref[a, 0:n, :] = q * masks[a:a + 1, :]

    def scores(unit, keys):
        a, c0, nc = unit
        return lax.dot_general(keys, qs_ref[a, c0:c0 + nc, :], _NT, preferred_element_type=F32)

    def values(a, k0, nk):
        return vt_ref[a * VT_ROWS:(a + 1) * VT_ROWS, pl.ds(k0, nk)]

    def visible(shape, c0, causal_lead):
        row = lax.broadcasted_iota(jnp.int32, shape, 0)
        col = lax.broadcasted_iota(jnp.int32, shape, 1)
        lead = row < FRONT
        lowest = jnp.where(lead, PAD, FRONT)
        highest = jnp.where(lead, col if causal_lead else FRONT, col + (c0 + FRONT))
        return (row >= lowest) & (row <= highest)

    init(FRONT)
    set_queries(0, FRONT)
    lead_units = [(0, 0, FRONT), (1, 0, FRONT)]
    lead_keys = k_ref[0:FRONT, :]
    lead_sts = [scores(u, lead_keys) for u in lead_units]
    for u, st in zip(lead_units, lead_sts):
        update(u, st, values(u[0], 0, FRONT), visible(st.shape, 0, True))
    finish(0, FRONT)

    n_sub = tq // QUERY_SUB
    units = [(a, s * QUERY_SUB, QUERY_SUB) for a in range(2) for s in range(n_sub)]

    def block_start(j):
        return pl.multiple_of(FRONT + j * tq, LANES)

    def score_block(slot, j):
        keys = k_ref[pl.ds(block_start(j), tq), :]
        for u, unit in enumerate(units):
            s_ref[slot, u] = scores(unit, keys)

    def consume_block(slot, j):
        for u, unit in enumerate(units):
            update(unit, s_ref[slot, u], values(unit[0], block_start(j), tq), None)

    def q_block(i, carry):
        r0 = block_start(i)
        set_queries(r0, tq)
        init(tq)

        @pl.when(i > 0)
        def _():
            score_block(0, 0)

        def pair(t, c):
            score_block(1, 2 * t + 1)
            consume_block(0, 2 * t)
            score_block(0, jnp.minimum(2 * t + 2, i - 1))
            consume_block(1, 2 * t + 1)
            return c

        lax.fori_loop(0, i // 2, pair, 0)

        @pl.when(i % 2 == 1)
        def _():
            consume_block(0, i - 1)

        kk = jnp.concatenate([k_ref[0:FRONT, :], k_ref[pl.ds(r0, tq), :]], axis=0)
        vts = [jnp.concatenate([values(a, 0, FRONT), values(a, r0, tq)], axis=1) for a in range(2)]
        n_keys = [FRONT + unit[1] + QUERY_SUB for unit in units]
        sts = [scores(unit, kk[0:nk, :]) for unit, nk in zip(units, n_keys)]
        for unit, nk, st in zip(units, n_keys, sts):
            update(unit, st, vts[unit[0]][:, 0:nk], visible(st.shape, unit[1], False))
        finish(r0, tq)
        return carry

    lax.fori_loop(0, n_blocks, q_block, 0)


def _causal_attn(q, k, vt, masks, tq):
    b, _, lp = vt.shape
    n_pairs = N_HEADS // 2
    n_blocks = (lp - FRONT) // tq
    assert tq % QUERY_SUB == 0

    def qk_spec():
        return pl.BlockSpec((None, lp, 2 * LANES), lambda bi, p: (bi, 0, p))

    return pl.pallas_call(
        functools.partial(_causal_attn_kernel, tq=tq, n_blocks=n_blocks),
        grid=(b, n_pairs),
        in_specs=[pl.BlockSpec((None, 2, 2 * LANES), lambda bi, p: (p, 0, 0)),
                  qk_spec(), qk_spec(),
                  pl.BlockSpec((None, 2 * VT_ROWS, lp), lambda bi, p: (bi, p, 0))],
        out_specs=pl.BlockSpec((None, lp, LANES), lambda bi, p: (bi, 0, p)),
        out_shape=jax.ShapeDtypeStruct((b, lp, BRANCH_WIDTH), BF16),
        scratch_shapes=[pltpu.VMEM((2, 1, tq), F32), pltpu.VMEM((2, VT_ROWS, tq), F32),
                        pltpu.VMEM((2, tq, 2 * LANES), BF16),
                        pltpu.VMEM((2, 2 * (tq // QUERY_SUB), tq, QUERY_SUB), F32)],
        compiler_params=pltpu.CompilerParams(dimension_semantics=("arbitrary", "arbitrary"),
                                             vmem_limit_bytes=VMEM_LIMIT),
        name="causal_attn",
    )(masks, q, k, vt)


def _swa_kernel(sinks_ref, mask_ref, q_ref, qaux_ref, k_ref, kaux_ref, vt_ref, o_ref, bias_ref, *, qb, n_blocks):
    masks = mask_ref[...]

    def key_bias(shape, band_shift, causal_front):
        row = lax.broadcasted_iota(jnp.int32, shape, 0)
        col = lax.broadcasted_iota(jnp.int32, shape, 1)
        in_front = row < FRONT
        lowest = jnp.where(in_front, PAD, col + (band_shift + 1))
        highest = jnp.where(in_front, col if causal_front else FRONT, col + (band_shift + WINDOW))
        return jnp.where((row >= lowest) & (row <= highest), 0.0, NEG)

    def attend(r0, nq, key_slices, bias):
        kk = jnp.concatenate([jnp.concatenate([k_ref[pl.ds(s, n), :], kaux_ref[pl.ds(s, n), :]], axis=1)
                              for s, n in key_slices], axis=0)
        vts = [jnp.concatenate([vt_ref[g * VT_ROWS:(g + 1) * VT_ROWS, pl.ds(s, n)] for s, n in key_slices], axis=1)
               for g in range(SWA_KV_HEADS)]
        sts = []
        for t in range(4):
            cols = slice(t * LANES, (t + 1) * LANES)
            qt = jnp.concatenate([q_ref[pl.ds(r0, nq), cols], qaux_ref[pl.ds(r0, nq), cols]], axis=1)
            sts += [lax.dot_general(kk, qt * masks[a:a + 1, :], _NT, preferred_element_type=F32) for a in range(2)]
        for t in range(4):
            cols = slice(t * LANES, (t + 1) * LANES)
            outs = []
            for a in range(2):
                sink = sinks_ref[t + 4 * a]
                st = sts[2 * t + a] + bias
                m = jnp.maximum(jnp.max(st, axis=0, keepdims=True), sink)
                p = jnp.exp(st - m).astype(BF16)
                acc = jnp.dot(vts[a], p, preferred_element_type=F32)
                den = acc[HEAD_DIM:HEAD_DIM + 1, :] + jnp.exp(sink - m)
                outs.append(acc[0:HEAD_DIM, :] * (1.0 / den))
            o_ref[pl.ds(r0, nq), cols] = jnp.concatenate(outs, axis=0).T.astype(o_ref.dtype)

    attend(0, FRONT, [(0, FRONT)], key_bias((FRONT, FRONT), 0, True))
    attend(FRONT, qb, [(0, FRONT + qb)], key_bias((FRONT + qb, qb), 0, False))
    bias_ref[...] = key_bias(bias_ref.shape, FRONT, False)

    def q_block(i, carry):
        r0 = pl.multiple_of(FRONT + i * qb, LANES)
        attend(r0, qb, [(0, FRONT), (pl.multiple_of(r0 - WINDOW, LANES), WINDOW + qb)], bias_ref[...])
        return carry

    lax.fori_loop(1, n_blocks, q_block, 0)


def _swa_attn(q, k, vt, sinks, consts):
    b, lp, _ = q.shape
    qb = SWA_QUERY_BLOCK
    assert (lp - FRONT) % qb == 0
    full = lambda a: pl.BlockSpec(a.shape, lambda bi: (0,) * a.ndim)
    per_batch = lambda r, c: pl.BlockSpec((None, r, c), lambda bi: (bi, 0, 0))
    return pl.pallas_call(
        functools.partial(_swa_kernel, qb=qb, n_blocks=(lp - FRONT) // qb),
        grid=(b,),
        in_specs=[pl.BlockSpec(memory_space=pltpu.SMEM), full(consts['swa_masks']),
                  per_batch(lp, BRANCH_WIDTH), full(consts['swa_qaux']),
                  per_batch(lp, LANES), full(consts['swa_kaux']),
                  per_batch(SWA_KV_HEADS * VT_ROWS, lp)],
        out_specs=per_batch(lp, BRANCH_WIDTH),
        out_shape=jax.ShapeDtypeStruct((b, lp, BRANCH_WIDTH), BF16),
        scratch_shapes=[pltpu.VMEM((FRONT + WINDOW + qb, qb), F32)],
        compiler_params=pltpu.CompilerParams(dimension_semantics=("arbitrary",), vmem_limit_bytes=VMEM_LIMIT),
        name="swa_attn",
    )(sinks, consts['swa_masks'], q, consts['swa_qaux'], k, consts['swa_kaux'], vt)


def _merge_kernel(h_ref, oa_ref, ob_ref, oc_ref, g1_ref, wg_ref, wb_ref, wo_ref, out_ref):
    h = h_ref[...]
    d = h.shape[-1]
    xn = (_rms_rows(h, d) * g1_ref[...]).astype(BF16)
    merged = None
    for n, o_ref in enumerate((oa_ref, ob_ref, oc_ref)):
        gate = jax.nn.sigmoid(jnp.dot(xn, wg_ref[:, n * d:(n + 1) * d], preferred_element_type=F32))
        y = jnp.dot(o_ref[...], wb_ref[n], preferred_element_type=F32)
        merged = gate * y if merged is None else merged + gate * y
    out_ref[...] = h + jnp.dot(merged.astype(BF16), wo_ref[...], preferred_element_type=F32)


def _merge(h, oa, ob, oc, p, tm):
    b, lp, d = h.shape
    row_blk = lambda w: pl.BlockSpec((None, tm, w), lambda bi, j: (bi, j, 0))
    full = lambda a: pl.BlockSpec(a.shape, lambda bi, j: (0,) * a.ndim)
    ws = [p['g1'], p['wg'], p['wb'], p['wo']]
    return pl.pallas_call(
        _merge_kernel,
        grid=(b, lp // tm),
        in_specs=[row_blk(d), row_blk(BRANCH_WIDTH), row_blk(BRANCH_WIDTH), row_blk(BRANCH_WIDTH)] + [full(a) for a in ws],
        out_specs=row_blk(d),
        out_shape=jax.ShapeDtypeStruct(h.shape, F32),
        compiler_params=pltpu.CompilerParams(dimension_semantics=("arbitrary", "arbitrary"),
                                             vmem_limit_bytes=VMEM_LIMIT),
        name="merge",
    )(h, oa, ob, oc, *ws)


_FF_CHUNK = 256
_HALO = 8


def _ffn_kernel(h_ref, g2_ref, wup_ref, cw_ref, cb_ref, wdn_ref, out_ref, carry_ref, ubuf_ref, act_ref, *, tm, d_ff):
    j = pl.program_id(1)
    h = h_ref[...]
    xn = _rms_rows(h, h.shape[-1]) * g2_ref[...]
    row = j * tm + lax.broadcasted_iota(jnp.int32, (tm, 1), 0)
    xn = jnp.where(row >= PAD, xn, 0.0).astype(BF16)

    @pl.when(j == 0)
    def _():
        carry_ref[...] = jnp.zeros_like(carry_ref)

    for c in range(d_ff // _FF_CHUNK):
        acts = []
        for half in range(2):
            c0 = half * d_ff + c * _FF_CHUNK
            cols = slice(c0, c0 + _FF_CHUNK)
            u = jnp.dot(xn, wup_ref[:, cols], preferred_element_type=F32)
            ubuf_ref[half, 0:_HALO, :] = carry_ref[:, cols]
            ubuf_ref[half, _HALO:_HALO + tm, :] = u
            carry_ref[:, cols] = u[tm - _HALO:tm, :]
            u1 = ubuf_ref[half, _HALO - 1:_HALO - 1 + tm, :]
            u2 = ubuf_ref[half, _HALO - 2:_HALO - 2 + tm, :]
            acts.append(cb_ref[:, cols] + cw_ref[2:3, cols] * u + cw_ref[1:2, cols] * u1 + cw_ref[0:1, cols] * u2)
        gate, val = acts
        act_ref[:, c * _FF_CHUNK:(c + 1) * _FF_CHUNK] = (gate * jax.nn.sigmoid(gate) * val).astype(BF16)
    out_ref[...] = h + jnp.dot(act_ref[...], wdn_ref[...], preferred_element_type=F32)


def _ffn(h, p, tm):
    b, lp, d = h.shape
    d_ff = p['wdn'].shape[0]
    row_blk = pl.BlockSpec((None, tm, d), lambda bi, j: (bi, j, 0))
    full = lambda a: pl.BlockSpec(a.shape, lambda bi, j: (0,) * a.ndim, pipeline_mode=pl.Buffered(1))
    ws = [p['g2'], p['wup'], p['cw'], p['cb'], p['wdn']]
    return pl.pallas_call(
        functools.partial(_ffn_kernel, tm=tm, d_ff=d_ff),
        grid=(b, lp // tm),
        in_specs=[row_blk] + [full(a) for a in ws],
        out_specs=row_blk,
        out_shape=jax.ShapeDtypeStruct(h.shape, F32),
        scratch_shapes=[pltpu.VMEM((_HALO, 2 * d_ff), F32),
                        pltpu.VMEM((2, _HALO + tm, _FF_CHUNK), F32),
                        pltpu.VMEM((tm, d_ff), BF16)],
        compiler_params=pltpu.CompilerParams(dimension_semantics=("arbitrary", "arbitrary"),
                                             vmem_limit_bytes=VMEM_LIMIT),
        name="conv_ffn",
    )(h, *ws)


def _constants(lp, tm):
    half = ROPE_DIM // 2
    freqs = ROPE_THETA ** (-np.arange(half, dtype=np.float64) / half)
    lane = np.arange(LANES)
    ang = (np.arange(lp) - PAD).astype(np.float64)[:, None] * freqs[lane % half][None, :]
    sign = np.where((lane % ROPE_DIM) < half, -1.0, 1.0)
    row = np.arange(lp)
    pos_hi, pos_lo = row // LANES, row % LANES
    swa_kaux = np.zeros((lp, LANES), np.float32)
    swa_qaux = np.zeros((lp, BRANCH_WIDTH), np.float32)
    swa_masks = np.zeros((2, 2 * LANES), np.float32)
    for a in range(2):
        swa_masks[a, a * HEAD_DIM:(a + 1) * HEAD_DIM] = 1.0
        swa_masks[a, LANES + 4 * a:LANES + 4 * (a + 1)] = 1.0
        swa_kaux[:, 4 * a:4 * (a + 1)] = np.stack([np.ones(lp), np.ones(lp), pos_hi, pos_lo], axis=1)
        for t in range(4):
            slope = 2.0 ** (-8.0 * (t + 4 * a + 1) / N_HEADS)
            swa_qaux[:, t * LANES + 4 * a:t * LANES + 4 * (a + 1)] = np.stack(
                [-LANES * slope * pos_hi, -slope * pos_lo, np.full(lp, LANES * slope), np.full(lp, slope)], axis=1)
    tri = np.tril(np.ones((tm, tm), np.float32))
    eq = np.zeros((DEC_PARTS, LANES, LANES), np.float32)
    ek = np.zeros((DEC_PARTS, LANES, LANES), np.float32)
    qconst = np.zeros((1, LANES), np.float32)
    kconst = np.zeros((1, LANES), np.float32)
    for hd in range(N_HEADS):
        base = 2 * DEC_PARTS * hd
        for i in range(DEC_PARTS):
            eq[i, hd, base + i] = 1.0
            ek[i, hd, base + DEC_PARTS + i] = -1.0
            qconst[0, base + DEC_PARTS + i] = 1.0
            kconst[0, base + i] = 1.0
    fox_masks = np.zeros((N_HEADS // 2, 2, 2 * LANES), np.float32)
    mla_masks = np.zeros((N_HEADS // 2, 2, 2 * LANES), np.float32)
    for p in range(N_HEADS // 2):
        for a in range(2):
            hd = 2 * p + a
            fox_masks[p, a, a * HEAD_DIM:(a + 1) * HEAD_DIM] = 1.0
            fox_masks[p, a, LANES + 2 * DEC_PARTS * hd:LANES + 2 * DEC_PARTS * (hd + 1)] = 1.0
            mla_masks[p, a, a * HEAD_DIM:(a + 1) * HEAD_DIM] = 1.0
            g = hd % 4
            mla_masks[p, a, LANES + g * ROPE_DIM:LANES + (g + 1) * ROPE_DIM] = 1.0
    return {
        'cos': jnp.asarray(np.cos(ang), F32), 'sin': jnp.asarray(np.sin(ang) * sign[None, :], F32),
        'tri': jnp.asarray(tri, BF16), 'eq': jnp.asarray(eq, BF16), 'ek': jnp.asarray(ek, BF16),
        'qconst': jnp.asarray(qconst), 'kconst': jnp.asarray(kconst),
        'fox_masks': jnp.asarray(fox_masks, BF16), 'mla_masks': jnp.asarray(mla_masks, BF16),
        'swa_masks': jnp.asarray(swa_masks, BF16), 'swa_qaux': jnp.asarray(swa_qaux, BF16),
        'swa_kaux': jnp.asarray(swa_kaux, BF16),
    }


def _swa_head_perm():
    cols = []
    for t in range(4):
        for hd in (t, 4 + t):
            cols += list(range(hd * HEAD_DIM, (hd + 1) * HEAD_DIM))
    return np.asarray(cols)


def _layer_params(l, norm1_g, w_in, fox_forget_b, fox_q_g, fox_k_g, mla_q_a_g, mla_w_q_up, mla_kv_a_g,
                  mla_w_kv_up, mla_q_g, mla_k_g, swa_q_g, swa_k_g, swa_sinks, w_branch, w_o,
                  norm2_g, ffn_w_up, ffn_conv_w, ffn_conv_b, ffn_w_down):
    d = w_in.shape[1]
    w = w_in[l]
    o_fq, o_fk, o_fv, o_ff = 0, 512, 1024, 1536
    o_cq = o_ff + N_HEADS
    o_ckv = o_cq + MLA_Q_RANK
    o_kr = o_ckv + MLA_KV_RANK
    o_sq = o_kr + ROPE_DIM
    o_sk = o_sq + 512
    o_sv = o_sk + SWA_KV_HEADS * HEAD_DIM
    o_g = o_sv + SWA_KV_HEADS * HEAD_DIM
    perm = _swa_head_perm()
    wa = jnp.concatenate([
        w[:, o_fq:o_fq + 512], w[:, o_fk:o_fk + 512],
        w[:, o_sq:o_sq + 512][:, perm], w[:, o_sk:o_sk + 128],
        w[:, o_cq:o_cq + MLA_Q_RANK], w[:, o_ckv:o_ckv + MLA_KV_RANK],
        jnp.tile(w[:, o_kr:o_kr + ROPE_DIM], (1, LANES // ROPE_DIM)),
        w[:, o_ff:o_ff + N_HEADS], jnp.zeros((d, LANES - N_HEADS), w.dtype)], axis=1).astype(BF16)
    tile_row = lambda g, n: jnp.tile(g, n)[None, :].astype(F32)
    width = HEAD_DIM + ROPE_DIM
    qcols = np.concatenate([np.concatenate([np.arange(hd * width, hd * width + HEAD_DIM) for hd in range(N_HEADS)]),
                            np.concatenate([np.arange(hd * width + HEAD_DIM, (hd + 1) * width) for hd in range(N_HEADS)])])
    kcols = np.concatenate([np.arange(hd * 2 * HEAD_DIM, hd * 2 * HEAD_DIM + HEAD_DIM) for hd in range(N_HEADS)])
    vcols = kcols + HEAD_DIM
    fb = jnp.concatenate([fox_forget_b[l], jnp.zeros((LANES - N_HEADS,), F32)])[None, :]
    wb = w_branch[l]
    wb = jnp.stack([wb[0], wb[1], wb[2][perm, :]]).astype(BF16)
    return {
        'g1': norm1_g[l][None, :], 'wa': wa,
        'gfq': tile_row(fox_q_g[l], N_HEADS) * (LOG2E * HEAD_DIM ** -0.5), 'gfk': tile_row(fox_k_g[l], N_HEADS),
        'gsq': tile_row(swa_q_g[l], N_HEADS) * (HEAD_DIM ** -0.5), 'gsk': tile_row(swa_k_g[l], SWA_KV_HEADS),
        'fb': fb,
        'gcq': mla_q_a_g[l][None, :], 'wqu': mla_w_q_up[l][:, qcols].astype(BF16),
        'gckv': mla_kv_a_g[l][None, :], 'wkvu': mla_w_kv_up[l][:, kcols].astype(BF16),
        'wvtf': w[:, o_fv:o_fv + 512].T.astype(BF16), 'wvtm': mla_w_kv_up[l][:, vcols].T.astype(BF16),
        'wvts': w[:, o_sv:o_sv + SWA_KV_HEADS * HEAD_DIM].T.astype(BF16),
        'gmq_n': tile_row(mla_q_g[l][:HEAD_DIM], N_HEADS) * (LOG2E * width ** -0.5),
        'gmq_r': tile_row(mla_q_g[l][HEAD_DIM:], LANES // ROPE_DIM) * (LOG2E * width ** -0.5),
        'gmk_n': tile_row(mla_k_g[l][:HEAD_DIM], N_HEADS),
        'gmk_r': tile_row(mla_k_g[l][HEAD_DIM:], LANES // ROPE_DIM),
        'sinks': swa_sinks[l].astype(F32),
        'wg': w[:, o_g:].astype(BF16), 'wb': wb, 'wo': w_o[l].astype(BF16),
        'g2': norm2_g[l][None, :], 'wup': ffn_w_up[l].astype(BF16), 'cw': ffn_conv_w[l], 'cb': ffn_conv_b[l][None, :],
        'wdn': ffn_w_down[l].astype(BF16),
    }


def kernel(x, meta_tokens, norm1_g, w_in, fox_forget_b, fox_q_g, fox_k_g, mla_q_a_g, mla_w_q_up, mla_kv_a_g, mla_w_kv_up, mla_q_g, mla_k_g, swa_q_g, swa_k_g, swa_sinks, w_branch, w_o, norm2_g, ffn_w_up, ffn_conv_w, ffn_conv_b, ffn_w_down):
    b, seq, d = x.shape
    assert seq % LANES == 0 and meta_tokens.shape == (N_META, d)
    lp = FRONT + seq
    tm = _row_tile(lp)
    tq = _row_tile(seq)
    consts = _constants(lp, tm)
    front = jnp.concatenate([jnp.zeros((PAD, d), x.dtype), meta_tokens.astype(x.dtype)], axis=0)
    h = jnp.concatenate([jnp.broadcast_to(front[None], (b, FRONT, d)), x], axis=1)
    weights = (norm1_g, w_in, fox_forget_b, fox_q_g, fox_k_g, mla_q_a_g, mla_w_q_up, mla_kv_a_g, mla_w_kv_up,
               mla_q_g, mla_k_g, swa_q_g, swa_k_g, swa_sinks, w_branch, w_o, norm2_g, ffn_w_up, ffn_conv_w,
               ffn_conv_b, ffn_w_down)
    for l in range(w_in.shape[0]):
        p = _layer_params(l, *weights)
        qf, kf, vtf, qm, km, vtm, sq, sk, vts = _in_proj(h, p, consts, tm)
        out_a = _causal_attn(qf, kf, vtf, consts['fox_masks'], tq)
        out_b = _causal_attn(qm, km, vtm, consts['mla_masks'], tq)
        out_c = _swa_attn(sq, sk, vts, p['sinks'], consts)
        h = _merge(h, out_a, out_b, out_c, p, tm)
        h = _ffn(h, p, tm)
    return h[:, FRONT:]
```

```python
import functools
import math

import numpy as np
import jax
import jax.numpy as jnp
from jax import lax
from jax.experimental import pallas as pl
from jax.experimental.pallas import tpu as pltpu

F32 = jnp.float32
BF16 = jnp.bfloat16

N_META = 16
EPS = 1e-6
N_HEADS = 8
HEAD_DIM = 64
ROPE_DIM = 32
MLA_Q_RANK = 256
MLA_KV_RANK = 128
SWA_KV_HEADS = 2
WINDOW = 128
ROPE_THETA = 10000.0
BRANCH_WIDTH = N_HEADS * HEAD_DIM
N_BRANCH = 3
CONV_WIDTH = 3

LANES = 128
FRONT = 128
PAD = FRONT - N_META
NEG = -1e30
DEC_PARTS = 3
LOG2E = math.log2(math.e)
ONES_ROWS = 16
VT_ROWS = HEAD_DIM + ONES_ROWS
QUERY_SUB = 256
PREFETCH = 2
SWA_QUERY_BLOCK = 256
VMEM_LIMIT = 56 * 1024 * 1024

_A_FQ, _A_FK, _A_SQ, _A_SK, _A_CQ, _A_CKV, _A_KR, _A_FF = 0, 512, 1024, 1536, 1664, 1920, 2048, 2176
_NT = (((1,), (1,)), ((), ()))


def _row_tile(lp):
    best = 128
    for t in range(128, 513, 128):
        if lp % t == 0:
            best = t
    return best


def _lane_iota(shape):
    return lax.broadcasted_iota(jnp.int32, shape, len(shape) - 1)


def _rms_rows(y, width):
    ss = jnp.sum(y * y, axis=-1, keepdims=True)
    return y * lax.rsqrt(ss * (1.0 / width) + EPS)


def _half_sums(y2):
    lane = _lane_iota(y2.shape)
    lo = jnp.sum(jnp.where(lane < HEAD_DIM, y2, 0.0), axis=-1, keepdims=True)
    hi = jnp.sum(jnp.where(lane >= HEAD_DIM, y2, 0.0), axis=-1, keepdims=True)
    return lo, hi


def _headnorm64_tile(yt):
    lo, hi = _half_sums(yt * yt)
    lane = _lane_iota(yt.shape)
    r = jnp.where(lane < HEAD_DIM, lax.rsqrt(lo * (1.0 / HEAD_DIM) + EPS), lax.rsqrt(hi * (1.0 / HEAD_DIM) + EPS))
    return yt * r


def _rotate_half(x):
    lane = _lane_iota(x.shape)
    half = ROPE_DIM // 2
    return jnp.where((lane % ROPE_DIM) < half, pltpu.roll(x, LANES - half, 1), pltpu.roll(x, half, 1))


def _quad_select(lane, vals):
    g = lane // ROPE_DIM
    out = jnp.where(g == 0, vals[0], vals[1])
    out = jnp.where(g == 2, vals[2], out)
    return jnp.where(g == 3, vals[3], out)


def _in_proj_kernel(h_ref, g1_ref, wa_ref, gfq_ref, gfk_ref, gsq_ref, gsk_ref, fb_ref,
                    gcq_ref, wqu_ref, gckv_ref, wkvu_ref, gmq_n_ref, gmq_r_ref, gmk_n_ref, gmk_r_ref,
                    wvtf_ref, wvtm_ref, wvts_ref,
                    cos_ref, sin_ref, tri_ref, eq_ref, ek_ref, qconst_ref, kconst_ref,
                    qf_ref, kf_ref, vtf_ref, qm_ref, km_ref, vtm_ref, sq_ref, sk_ref, vts_ref,
                    carry_ref, *, tm):
    j = pl.program_id(1)
    h = h_ref[...]
    xn = (_rms_rows(h, h.shape[-1]) * g1_ref[...]).astype(BF16)

    def proj(c0, width):
        return jnp.dot(xn, wa_ref[:, c0:c0 + width], preferred_element_type=F32)

    def store_vt(vt_ref, wt_ref, x):
        yt = lax.dot_general(wt_ref[...], x, _NT, preferred_element_type=F32)
        ones = jnp.ones((ONES_ROWS, tm), BF16)
        for hd in range(yt.shape[0] // HEAD_DIM):
            vt_ref[hd * VT_ROWS:hd * VT_ROWS + HEAD_DIM, :] = yt[hd * HEAD_DIM:(hd + 1) * HEAD_DIM, :].astype(BF16)
            vt_ref[hd * VT_ROWS + HEAD_DIM:(hd + 1) * VT_ROWS, :] = ones

    yq = proj(_A_FQ, 512)
    yk = proj(_A_FK, 512)
    for t in range(4):
        sl = slice(t * LANES, (t + 1) * LANES)
        qf_ref[:, 2 * t * LANES:(2 * t + 1) * LANES] = (_headnorm64_tile(yq[:, sl]) * gfq_ref[:, sl]).astype(BF16)
        kf_ref[:, 2 * t * LANES:(2 * t + 1) * LANES] = (_headnorm64_tile(yk[:, sl]) * gfk_ref[:, sl]).astype(BF16)
    store_vt(vtf_ref, wvtf_ref, xn)

    @pl.when(j == 0)
    def _():
        carry_ref[...] = jnp.zeros_like(carry_ref)

    z = proj(_A_FF, LANES) + fb_ref[...]
    ls = jnp.minimum(z, 0.0) - jnp.log1p(jnp.exp(-jnp.abs(z)))
    row = j * tm + lax.broadcasted_iota(jnp.int32, ls.shape, 0)
    ls = jnp.where((_lane_iota(ls.shape) < N_HEADS) & (row >= PAD), ls * LOG2E, 0.0)
    tri = tri_ref[...]
    c = carry_ref[0:1, :]
    rem = ls
    for _ in range(DEC_PARTS):
        part = rem.astype(BF16)
        c = c + jnp.dot(tri, part, preferred_element_type=F32)
        rem = rem - part.astype(F32)
    carry_ref[0:1, :] = c[tm - 1:tm, :]
    qd = qconst_ref[...]
    kd = kconst_ref[...]
    rem = c
    for i in range(DEC_PARTS):
        part = rem.astype(BF16)
        qd = qd + jnp.dot(part, eq_ref[i], preferred_element_type=F32)
        kd = kd + jnp.dot(part, ek_ref[i], preferred_element_type=F32)
        rem = rem - part.astype(F32)
    qd = qd.astype(BF16)
    kd = kd.astype(BF16)
    for t in range(4):
        qf_ref[:, (2 * t + 1) * LANES:(2 * t + 2) * LANES] = qd
        kf_ref[:, (2 * t + 1) * LANES:(2 * t + 2) * LANES] = kd

    ysq = proj(_A_SQ, 512)
    for t in range(4):
        sl = slice(t * LANES, (t + 1) * LANES)
        sq_ref[:, sl] = (_headnorm64_tile(ysq[:, sl]) * gsq_ref[:, sl]).astype(BF16)
    sk_ref[...] = (_headnorm64_tile(proj(_A_SK, LANES)) * gsk_ref[...]).astype(BF16)
    store_vt(vts_ref, wvts_ref, xn)

    cos = cos_ref[...]
    sin = sin_ref[...]

    def rope(x):
        return x * cos + _rotate_half(x) * sin

    cq = (_rms_rows(proj(_A_CQ, MLA_Q_RANK), MLA_Q_RANK) * gcq_ref[...]).astype(BF16)
    yq = jnp.dot(cq, wqu_ref[...], preferred_element_type=F32)
    lane = _lane_iota((tm, LANES))
    quad_id = lane // ROPE_DIM
    ss = []
    for t in range(4):
        lo, hi = _half_sums(jnp.square(yq[:, t * LANES:(t + 1) * LANES]))
        ss += [lo, hi]
    for u in range(2):
        y2 = jnp.square(yq[:, 512 + u * LANES:512 + (u + 1) * LANES])
        for g in range(4):
            ss[4 * u + g] = ss[4 * u + g] + jnp.sum(jnp.where(quad_id == g, y2, 0.0), axis=-1, keepdims=True)
    width = HEAD_DIM + ROPE_DIM
    rq = [lax.rsqrt(s * (1.0 / width) + EPS) for s in ss]
    q_rope = []
    for u in range(2):
        sl = slice(512 + u * LANES, 512 + (u + 1) * LANES)
        x = yq[:, sl] * _quad_select(lane, rq[4 * u:4 * u + 4]) * gmq_r_ref[...]
        q_rope.append(rope(x).astype(BF16))
    for t in range(4):
        sl = slice(t * LANES, (t + 1) * LANES)
        r = jnp.where(lane < HEAD_DIM, rq[2 * t], rq[2 * t + 1])
        qm_ref[:, 2 * t * LANES:(2 * t + 1) * LANES] = (yq[:, sl] * r * gmq_n_ref[:, sl]).astype(BF16)
        qm_ref[:, (2 * t + 1) * LANES:(2 * t + 2) * LANES] = q_rope[t // 2]

    ckv = (_rms_rows(proj(_A_CKV, MLA_KV_RANK), MLA_KV_RANK) * gckv_ref[...]).astype(BF16)
    ykv = jnp.dot(ckv, wkvu_ref[...], preferred_element_type=F32)
    kr4 = proj(_A_KR, LANES)
    ss_rope = jnp.sum(kr4 * kr4, axis=-1, keepdims=True) * 0.25
    rk = []
    for t in range(4):
        lo, hi = _half_sums(jnp.square(ykv[:, t * LANES:(t + 1) * LANES]))
        rk += [lax.rsqrt((lo + ss_rope) * (1.0 / width) + EPS), lax.rsqrt((hi + ss_rope) * (1.0 / width) + EPS)]
    kr_base = rope(kr4 * gmk_r_ref[...])
    k_rope = [(kr_base * _quad_select(lane, rk[4 * u:4 * u + 4])).astype(BF16) for u in range(2)]
    for t in range(4):
        sl = slice(t * LANES, (t + 1) * LANES)
        r = jnp.where(lane < HEAD_DIM, rk[2 * t], rk[2 * t + 1])
        km_ref[:, 2 * t * LANES:(2 * t + 1) * LANES] = (ykv[:, sl] * r * gmk_n_ref[:, sl]).astype(BF16)
        km_ref[:, (2 * t + 1) * LANES:(2 * t + 2) * LANES] = k_rope[t // 2]
    store_vt(vtm_ref, wvtm_ref, ckv)


def _in_proj(h, p, consts, tm):
    b, lp, d = h.shape
    nt = lp // tm
    row_blk = lambda w: pl.BlockSpec((None, tm, w), lambda bi, j: (bi, j, 0))
    full = lambda a: pl.BlockSpec(a.shape, lambda bi, j: (0,) * a.ndim)
    tab = pl.BlockSpec((tm, LANES), lambda bi, j: (j, 0))
    ins = [h, p['g1'], p['wa'], p['gfq'], p['gfk'], p['gsq'], p['gsk'], p['fb'],
           p['gcq'], p['wqu'], p['gckv'], p['wkvu'], p['gmq_n'], p['gmq_r'], p['gmk_n'], p['gmk_r'],
           p['wvtf'], p['wvtm'], p['wvts']]
    in_specs = [row_blk(d)] + [full(a) for a in ins[1:]]
    ins += [consts['cos'], consts['sin'], consts['tri'], consts['eq'], consts['ek'], consts['qconst'], consts['kconst']]
    in_specs += [tab, tab] + [full(consts[k]) for k in ('tri', 'eq', 'ek', 'qconst', 'kconst')]
    outs = [('row', 1024), ('row', 1024), ('vt', N_HEADS), ('row', 1024), ('row', 1024), ('vt', N_HEADS),
            ('row', 512), ('row', 128), ('vt', SWA_KV_HEADS)]
    out_shape = [jax.ShapeDtypeStruct((b, lp, w) if kind == 'row' else (b, w * VT_ROWS, lp), BF16) for kind, w in outs]
    out_specs = [row_blk(w) if kind == 'row' else pl.BlockSpec((None, w * VT_ROWS, tm), lambda bi, j: (bi, 0, j))
                 for kind, w in outs]
    return pl.pallas_call(
        functools.partial(_in_proj_kernel, tm=tm),
        grid=(b, nt),
        in_specs=in_specs,
        out_specs=out_specs,
        out_shape=out_shape,
        scratch_shapes=[pltpu.VMEM((8, LANES), F32)],
        compiler_params=pltpu.CompilerParams(dimension_semantics=("arbitrary", "arbitrary"),
                                             vmem_limit_bytes=VMEM_LIMIT),
        name="in_proj",
    )(*ins)


def _causal_attn_kernel(mask_ref, q_ref, k_ref, vt_ref, o_ref, m_ref, acc_ref, qs_ref, s_ref, sd_ref, *, tq, n_blocks):
    masks = mask_ref[...]

    def init(n):
        m_ref[:, :, 0:n] = jnp.full((2, 1, n), NEG, F32)
        acc_ref[:, :, 0:n] = jnp.zeros((2, VT_ROWS, n), F32)

    def update(unit, st, vt, visible):
        a, c0, nc = unit
        if visible is not None:
            st = jnp.where(visible, st, NEG)
        cols = slice(c0, c0 + nc)
        m_prev = m_ref[a, :, cols]
        m_new = jnp.maximum(m_prev, jnp.max(st, axis=0, keepdims=True))
        alpha = jnp.exp2(m_prev - m_new)
        p = jnp.exp2(st - m_new).astype(BF16)
        acc_ref[a, :, cols] = alpha * acc_ref[a, :, cols] + jnp.dot(vt, p, preferred_element_type=F32)
        m_ref[a, :, cols] = m_new

    def finish(r0, n):
        ot = jnp.concatenate([acc_ref[a, 0:HEAD_DIM, 0:n] * (1.0 / acc_ref[a, HEAD_DIM:HEAD_DIM + 1, 0:n])
                              for a in range(2)], axis=0)
        o_ref[pl.ds(r0, n), :] = ot.T.astype(o_ref.dtype)

    def set_queries(qslot, r0, n):
        q = q_ref[pl.ds(r0, n), :]
        for a in range(2):
            qs_ref[qslot, a, 0:n, :] = q * masks[a:a + 1, :]

    def scores(qslot, unit, keys):
        a, c0, nc = unit
        return lax.dot_general(keys, qs_ref[qslot, a, c0:c0 + nc, :], _NT, preferred_element_type=F32)

    def values(a, k0, nk):
        return vt_ref[a * VT_ROWS:(a + 1) * VT_ROWS, pl.ds(k0, nk)]

    def visible(shape, c0, causal_lead):
        row = lax.broadcasted_iota(jnp.int32, shape, 0)
        col = lax.broadcasted_iota(jnp.int32, shape, 1)
        lead = row < FRONT
        lowest = jnp.where(lead, PAD, FRONT)
        highest = jnp.where(lead, col if causal_lead else FRONT, col + (c0 + FRONT))
        return (row >= lowest) & (row <= highest)

    init(FRONT)
    set_queries(0, 0, FRONT)
    lead_units = [(0, 0, FRONT), (1, 0, FRONT)]
    lead_keys = k_ref[0:FRONT, :]
    lead_sts = [scores(0, u, lead_keys) for u in lead_units]
    for u, st in zip(lead_units, lead_sts):
        update(u, st, values(u[0], 0, FRONT), visible(st.shape, 0, True))
    finish(0, FRONT)

    n_sub = tq // QUERY_SUB
    units = [(a, s * QUERY_SUB, QUERY_SUB) for a in range(2) for s in range(n_sub)]
    diag_keys = [FRONT + unit[1] + QUERY_SUB for unit in units]

    def block_start(j):
        return pl.multiple_of(FRONT + j * tq, LANES)

    def park_block(slot, qslot, j):
        keys = k_ref[pl.ds(block_start(j), tq), :]
        for u, unit in enumerate(units):
            s_ref[slot, u] = scores(qslot, unit, keys)

    def consume_block(slot, j):
        for u, unit in enumerate(units):
            update(unit, s_ref[slot, u], values(unit[0], block_start(j), tq), None)

    def park_diag(qslot, i):
        kk = jnp.concatenate([k_ref[0:FRONT, :], k_ref[pl.ds(block_start(i), tq), :]], axis=0)
        for u, (unit, nk) in enumerate(zip(units, diag_keys)):
            sd_ref[u, 0:nk, :] = scores(qslot, unit, kk[0:nk, :])

    def consume_diag(i):
        vts = [jnp.concatenate([values(a, 0, FRONT), values(a, block_start(i), tq)], axis=1) for a in range(2)]
        for u, (unit, nk) in enumerate(zip(units, diag_keys)):
            update(unit, sd_ref[u, 0:nk, :], vts[unit[0]][:, 0:nk], visible((nk, unit[2]), unit[1], False))

    def visible_pairs(qslot, i, n_pairs):
        def pair(t, c):
            park_block(1, qslot, 2 * t + 1)
            consume_block(0, 2 * t)
            park_block(0, qslot, jnp.minimum(2 * t + 2, i - 1))
            consume_block(1, 2 * t + 1)
            return c
        lax.fori_loop(0, n_pairs, pair, 0)

    set_queries(0, block_start(0), tq)

    def two_blocks(g, carry):
        even, odd = 2 * g, 2 * g + 1
        init(tq)
        visible_pairs(0, even, g)
        set_queries(1, block_start(odd), tq)
        park_diag(0, even)
        park_block(0, 1, 0)
        consume_diag(even)
        finish(block_start(even), tq)
        init(tq)
        visible_pairs(1, odd, g)
        park_diag(1, odd)
        consume_block(0, 2 * g)
        nxt = jnp.minimum(even + 2, n_blocks - 1)
        set_queries(0, block_start(nxt), tq)
        park_block(0, 0, 0)
        consume_diag(odd)
        finish(block_start(odd), tq)
        return carry

    lax.fori_loop(0, n_blocks // 2, two_blocks, 0)


def _causal_attn(q, k, vt, masks, tq):
    b, _, lp = vt.shape
    n_pairs = N_HEADS // 2
    n_blocks = (lp - FRONT) // tq
    assert tq % QUERY_SUB == 0 and n_blocks % 2 == 0
    n_units = 2 * (tq // QUERY_SUB)

    def qk_spec():
        return pl.BlockSpec((None, lp, 2 * LANES), lambda bi, p: (bi, 0, p))

    return pl.pallas_call(
        functools.partial(_causal_attn_kernel, tq=tq, n_blocks=n_blocks),
        grid=(b, n_pairs),
        in_specs=[pl.BlockSpec((None, 2, 2 * LANES), lambda bi, p: (p, 0, 0)),
                  qk_spec(), qk_spec(),
                  pl.BlockSpec((None, 2 * VT_ROWS, lp), lambda bi, p: (bi, p, 0))],
        out_specs=pl.BlockSpec((None, lp, LANES), lambda bi, p: (bi, 0, p)),
        out_shape=jax.ShapeDtypeStruct((b, lp, BRANCH_WIDTH), BF16),
        scratch_shapes=[pltpu.VMEM((2, 1, tq), F32), pltpu.VMEM((2, VT_ROWS, tq), F32),
                        pltpu.VMEM((2, 2, tq, 2 * LANES), BF16),
                        pltpu.VMEM((2, n_units, tq, QUERY_SUB), F32),
                        pltpu.VMEM((n_units, FRONT + tq, QUERY_SUB), F32)],
        compiler_params=pltpu.CompilerParams(dimension_semantics=("arbitrary", "arbitrary"),
                                             vmem_limit_bytes=VMEM_LIMIT),
        name="causal_attn",
    )(masks, q, k, vt)


def _swa_kernel(sinks_ref, mask_ref, q_ref, qaux_ref, k_ref, kaux_ref, vt_ref, o_ref, bias_ref, s_ref, *, qb, n_blocks):
    masks = mask_ref[...]
    n_band = WINDOW + qb

    def key_bias(shape, n_band_rows, band_shift):
        row = lax.broadcasted_iota(jnp.int32, shape, 0)
        col = lax.broadcasted_iota(jnp.int32, shape, 1)
        band = row < n_band_rows
        lowest = jnp.where(band, col + (band_shift + 1), 0)
        highest = jnp.where(band, col + (band_shift + WINDOW), n_band_rows + N_META)
        return jnp.where((row >= lowest) & (row <= highest), 0.0, NEG)

    def keys(key_slices):
        return jnp.concatenate([jnp.concatenate([k_ref[pl.ds(s, n), :], kaux_ref[pl.ds(s, n), :]], axis=1)
                                for s, n in key_slices], axis=0)

    def values(key_slices):
        return [jnp.concatenate([vt_ref[g * VT_ROWS:(g + 1) * VT_ROWS, pl.ds(s, n)] for s, n in key_slices], axis=1)
                for g in range(SWA_KV_HEADS)]

    def tile_scores(r0, nq, t, kk):
        cols = slice(t * LANES, (t + 1) * LANES)
        qt = jnp.concatenate([q_ref[pl.ds(r0, nq), cols], qaux_ref[pl.ds(r0, nq), cols]], axis=1)
        return [lax.dot_general(kk, qt * masks[a:a + 1, :], _NT, preferred_element_type=F32) for a in range(2)]

    def tile_finish(r0, nq, t, sts, vts, bias):
        outs = []
        for a in range(2):
            sink = sinks_ref[t + 4 * a]
            st = sts[a] + bias
            m = jnp.maximum(jnp.max(st, axis=0, keepdims=True), sink)
            p = jnp.exp(st - m).astype(BF16)
            acc = jnp.dot(vts[a], p, preferred_element_type=F32)
            den = acc[HEAD_DIM:HEAD_DIM + 1, :] + jnp.exp(sink - m)
            outs.append(acc[0:HEAD_DIM, :] * (1.0 / den))
        o_ref[pl.ds(r0, nq), t * LANES:(t + 1) * LANES] = jnp.concatenate(outs, axis=0).T.astype(o_ref.dtype)

    def attend(r0, nq, key_slices, bias):
        kk, vts = keys(key_slices), values(key_slices)
        for t in range(4):
            tile_finish(r0, nq, t, tile_scores(r0, nq, t, kk), vts, bias)

    row = lax.broadcasted_iota(jnp.int32, (FRONT, FRONT), 0)
    col = lax.broadcasted_iota(jnp.int32, (FRONT, FRONT), 1)
    attend(0, FRONT, [(0, FRONT)], jnp.where((row >= PAD) & (row <= col), 0.0, NEG))
    meta = (PAD, N_META)
    attend(FRONT, qb, [(FRONT, qb), meta], key_bias((qb + N_META, qb), qb, -WINDOW))
    bias_ref[...] = key_bias(bias_ref.shape, n_band, 0)

    def block_keys(i):
        return [(pl.multiple_of(FRONT + i * qb - WINDOW, LANES), n_band), meta]

    def park_scores(slot, i, t):
        sts = tile_scores(pl.multiple_of(FRONT + i * qb, LANES), qb, t, keys(block_keys(i)))
        for a in range(2):
            s_ref[slot, a] = sts[a]

    park_scores(0, 1, 0)

    def q_block(i, carry):
        r0 = pl.multiple_of(FRONT + i * qb, LANES)
        vts = values(block_keys(i))
        for t in range(4):
            if t < 3:
                park_scores((t + 1) % 2, i, t + 1)
            else:
                park_scores(0, jnp.minimum(i + 1, n_blocks - 1), 0)
            tile_finish(r0, qb, t, [s_ref[t % 2, a] for a in range(2)], vts, bias_ref[...])
        return carry

    lax.fori_loop(1, n_blocks, q_block, 0)


def _swa_attn(q, k, vt, sinks, consts):
    b, lp, _ = q.shape
    qb = SWA_QUERY_BLOCK
    assert (lp - FRONT) % qb == 0
    full = lambda a: pl.BlockSpec(a.shape, lambda bi: (0,) * a.ndim)
    per_batch = lambda r, c: pl.BlockSpec((None, r, c), lambda bi: (bi, 0, 0))
    return pl.pallas_call(
        functools.partial(_swa_kernel, qb=qb, n_blocks=(lp - FRONT) // qb),
        grid=(b,),
        in_specs=[pl.BlockSpec(memory_space=pltpu.SMEM), full(consts['swa_masks']),
                  per_batch(lp, BRANCH_WIDTH), full(consts['swa_qaux']),
                  per_batch(lp, LANES), full(consts['swa_kaux']),
                  per_batch(SWA_KV_HEADS * VT_ROWS, lp)],
        out_specs=per_batch(lp, BRANCH_WIDTH),
        out_shape=jax.ShapeDtypeStruct((b, lp, BRANCH_WIDTH), BF16),
        scratch_shapes=[pltpu.VMEM((WINDOW + qb + N_META, qb), F32),
                        pltpu.VMEM((2, 2, WINDOW + qb + N_META, qb), F32)],
        compiler_params=pltpu.CompilerParams(dimension_semantics=("arbitrary",), vmem_limit_bytes=VMEM_LIMIT),
        name="swa_attn",
    )(sinks, consts['swa_masks'], q, consts['swa_qaux'], k, consts['swa_kaux'], vt)


def _merge_kernel(h_ref, oa_ref, ob_ref, oc_ref, g1_ref, wg_ref, wb_ref, wo_ref, out_ref):
    h = h_ref[...]
    d = h.shape[-1]
    xn = (_rms_rows(h, d) * g1_ref[...]).astype(BF16)
    merged = None
    for n, o_ref in enumerate((oa_ref, ob_ref, oc_ref)):
        gate = jax.nn.sigmoid(jnp.dot(xn, wg_ref[:, n * d:(n + 1) * d], preferred_element_type=F32))
        y = jnp.dot(o_ref[...], wb_ref[n], preferred_element_type=F32)
        merged = gate * y if merged is None else merged + gate * y
    out_ref[...] = h + jnp.dot(merged.astype(BF16), wo_ref[...], preferred_element_type=F32)


def _merge(h, oa, ob, oc, p, tm):
    b, lp, d = h.shape
    row_blk = lambda w: pl.BlockSpec((None, tm, w), lambda bi, j: (bi, j, 0))
    full = lambda a: pl.BlockSpec(a.shape, lambda bi, j: (0,) * a.ndim)
    ws = [p['g1'], p['wg'], p['wb'], p['wo']]
    return pl.pallas_call(
        _merge_kernel,
        grid=(b, lp // tm),
        in_specs=[row_blk(d), row_blk(BRANCH_WIDTH), row_blk(BRANCH_WIDTH), row_blk(BRANCH_WIDTH)] + [full(a) for a in ws],
        out_specs=row_blk(d),
        out_shape=jax.ShapeDtypeStruct(h.shape, F32),
        compiler_params=pltpu.CompilerParams(dimension_semantics=("arbitrary", "arbitrary"),
                                             vmem_limit_bytes=VMEM_LIMIT),
        name="merge",
    )(h, oa, ob, oc, *ws)


_FF_CHUNK = 256
_HALO = 8


def _ffn_kernel(h_ref, g2_ref, wup_ref, cw_ref, cb_ref, wdn_ref, out_ref, carry_ref, ubuf_ref, act_ref, *, tm, d_ff):
    j = pl.program_id(1)
    h = h_ref[...]
    xn = _rms_rows(h, h.shape[-1]) * g2_ref[...]
    row = j * tm + lax.broadcasted_iota(jnp.int32, (tm, 1), 0)
    xn = jnp.where(row >= PAD, xn, 0.0).astype(BF16)

    @pl.when(j == 0)
    def _():
        carry_ref[...] = jnp.zeros_like(carry_ref)

    for c in range(d_ff // _FF_CHUNK):
        acts = []
        for half in range(2):
            c0 = half * d_ff + c * _FF_CHUNK
            cols = slice(c0, c0 + _FF_CHUNK)
            u = jnp.dot(xn, wup_ref[:, cols], preferred_element_type=F32)
            ubuf_ref[half, 0:_HALO, :] = carry_ref[:, cols]
            ubuf_ref[half, _HALO:_HALO + tm, :] = u
            carry_ref[:, cols] = u[tm - _HALO:tm, :]
            u1 = ubuf_ref[half, _HALO - 1:_HALO - 1 + tm, :]
            u2 = ubuf_ref[half, _HALO - 2:_HALO - 2 + tm, :]
            acts.append(cb_ref[:, cols] + cw_ref[2:3, cols] * u + cw_ref[1:2, cols] * u1 + cw_ref[0:1, cols] * u2)
        gate, val = acts
        act_ref[:, c * _FF_CHUNK:(c + 1) * _FF_CHUNK] = (gate * jax.nn.sigmoid(gate) * val).astype(BF16)
    out_ref[...] = h + jnp.dot(act_ref[...], wdn_ref[...], preferred_element_type=F32)


def _ffn(h, p, tm):
    b, lp, d = h.shape
    d_ff = p['wdn'].shape[0]
    row_blk = pl.BlockSpec((None, tm, d), lambda bi, j: (bi, j, 0))
    full = lambda a: pl.BlockSpec(a.shape, lambda bi, j: (0,) * a.ndim, pipeline_mode=pl.Buffered(1))
    ws = [p['g2'], p['wup'], p['cw'], p['cb'], p['wdn']]
    return pl.pallas_call(
        functools.partial(_ffn_kernel, tm=tm, d_ff=d_ff),
        grid=(b, lp // tm),
        in_specs=[row_blk] + [full(a) for a in ws],
        out_specs=row_blk,
        out_shape=jax.ShapeDtypeStruct(h.shape, F32),
        scratch_shapes=[pltpu.VMEM((_HALO, 2 * d_ff), F32),
                        pltpu.VMEM((2, _HALO + tm, _FF_CHUNK), F32),
                        pltpu.VMEM((tm, d_ff), BF16)],
        compiler_params=pltpu.CompilerParams(dimension_semantics=("arbitrary", "arbitrary"),
                                             vmem_limit_bytes=VMEM_LIMIT),
        name="conv_ffn",
    )(h, *ws)


def _constants(lp, tm):
    half = ROPE_DIM // 2
    freqs = ROPE_THETA ** (-np.arange(half, dtype=np.float64) / half)
    lane = np.arange(LANES)
    ang = (np.arange(lp) - PAD).astype(np.float64)[:, None] * freqs[lane % half][None, :]
    sign = np.where((lane % ROPE_DIM) < half, -1.0, 1.0)
    row = np.arange(lp)
    pos_hi, pos_lo = row // LANES, row % LANES
    swa_kaux = np.zeros((lp, LANES), np.float32)
    swa_qaux = np.zeros((lp, BRANCH_WIDTH), np.float32)
    swa_masks = np.zeros((2, 2 * LANES), np.float32)
    for a in range(2):
        swa_masks[a, a * HEAD_DIM:(a + 1) * HEAD_DIM] = 1.0
        swa_masks[a, LANES + 4 * a:LANES + 4 * (a + 1)] = 1.0
        swa_kaux[:, 4 * a:4 * (a + 1)] = np.stack([np.ones(lp), np.ones(lp), pos_hi, pos_lo], axis=1)
        for t in range(4):
            slope = 2.0 ** (-8.0 * (t + 4 * a + 1) / N_HEADS)
            swa_qaux[:, t * LANES + 4 * a:t * LANES + 4 * (a + 1)] = np.stack(
                [-LANES * slope * pos_hi, -slope * pos_lo, np.full(lp, LANES * slope), np.full(lp, slope)], axis=1)
    tri = np.tril(np.ones((tm, tm), np.float32))
    eq = np.zeros((DEC_PARTS, LANES, LANES), np.float32)
    ek = np.zeros((DEC_PARTS, LANES, LANES), np.float32)
    qconst = np.zeros((1, LANES), np.float32)
    kconst = np.zeros((1, LANES), np.float32)
    for hd in range(N_HEADS):
        base = 2 * DEC_PARTS * hd
        for i in range(DEC_PARTS):
            eq[i, hd, base + i] = 1.0
            ek[i, hd, base + DEC_PARTS + i] = -1.0
            qconst[0, base + DEC_PARTS + i] = 1.0
            kconst[0, base + i] = 1.0
    fox_masks = np.zeros((N_HEADS // 2, 2, 2 * LANES), np.float32)
    mla_masks = np.zeros((N_HEADS // 2, 2, 2 * LANES), np.float32)
    for p in range(N_HEADS // 2):
        for a in range(2):
            hd = 2 * p + a
            fox_masks[p, a, a * HEAD_DIM:(a + 1) * HEAD_DIM] = 1.0
            fox_masks[p, a, LANES + 2 * DEC_PARTS * hd:LANES + 2 * DEC_PARTS * (hd + 1)] = 1.0
            mla_masks[p, a, a * HEAD_DIM:(a + 1) * HEAD_DIM] = 1.0
            g = hd % 4
            mla_masks[p, a, LANES + g * ROPE_DIM:LANES + (g + 1) * ROPE_DIM] = 1.0
    return {
        'cos': jnp.asarray(np.cos(ang), F32), 'sin': jnp.asarray(np.sin(ang) * sign[None, :], F32),
        'tri': jnp.asarray(tri, BF16), 'eq': jnp.asarray(eq, BF16), 'ek': jnp.asarray(ek, BF16),
        'qconst': jnp.asarray(qconst), 'kconst': jnp.asarray(kconst),
        'fox_masks': jnp.asarray(fox_masks, BF16), 'mla_masks': jnp.asarray(mla_masks, BF16),
        'swa_masks': jnp.asarray(swa_masks, BF16), 'swa_qaux': jnp.asarray(swa_qaux, BF16),
        'swa_kaux': jnp.asarray(swa_kaux, BF16),
    }


def _swa_head_perm():
    cols = []
    for t in range(4):
        for hd in (t, 4 + t):
            cols += list(range(hd * HEAD_DIM, (hd + 1) * HEAD_DIM))
    return np.asarray(cols)


def _layer_params(l, norm1_g, w_in, fox_forget_b, fox_q_g, fox_k_g, mla_q_a_g, mla_w_q_up, mla_kv_a_g,
                  mla_w_kv_up, mla_q_g, mla_k_g, swa_q_g, swa_k_g, swa_sinks, w_branch, w_o,
                  norm2_g, ffn_w_up, ffn_conv_w, ffn_conv_b, ffn_w_down):
    d = w_in.shape[1]
    w = w_in[l]
    o_fq, o_fk, o_fv, o_ff = 0, 512, 1024, 1536
    o_cq = o_ff + N_HEADS
    o_ckv = o_cq + MLA_Q_RANK
    o_kr = o_ckv + MLA_KV_RANK
    o_sq = o_kr + ROPE_DIM
    o_sk = o_sq + 512
    o_sv = o_sk + SWA_KV_HEADS * HEAD_DIM
    o_g = o_sv + SWA_KV_HEADS * HEAD_DIM
    perm = _swa_head_perm()
    wa = jnp.concatenate([
        w[:, o_fq:o_fq + 512], w[:, o_fk:o_fk + 512],
        w[:, o_sq:o_sq + 512][:, perm], w[:, o_sk:o_sk + 128],
        w[:, o_cq:o_cq + MLA_Q_RANK], w[:, o_ckv:o_ckv + MLA_KV_RANK],
        jnp.tile(w[:, o_kr:o_kr + ROPE_DIM], (1, LANES // ROPE_DIM)),
        w[:, o_ff:o_ff + N_HEADS], jnp.zeros((d, LANES - N_HEADS), w.dtype)], axis=1).astype(BF16)
    tile_row = lambda g, n: jnp.tile(g, n)[None, :].astype(F32)
    width = HEAD_DIM + ROPE_DIM
    qcols = np.concatenate([np.concatenate([np.arange(hd * width, hd * width + HEAD_DIM) for hd in range(N_HEADS)]),
                            np.concatenate([np.arange(hd * width + HEAD_DIM, (hd + 1) * width) for hd in range(N_HEADS)])])
    kcols = np.concatenate([np.arange(hd * 2 * HEAD_DIM, hd * 2 * HEAD_DIM + HEAD_DIM) for hd in range(N_HEADS)])
    vcols = kcols + HEAD_DIM
    fb = jnp.concatenate([fox_forget_b[l], jnp.zeros((LANES - N_HEADS,), F32)])[None, :]
    wb = w_branch[l]
    wb = jnp.stack([wb[0], wb[1], wb[2][perm, :]]).astype(BF16)
    return {
        'g1': norm1_g[l][None, :], 'wa': wa,
        'gfq': tile_row(fox_q_g[l], N_HEADS) * (LOG2E * HEAD_DIM ** -0.5), 'gfk': tile_row(fox_k_g[l], N_HEADS),
        'gsq': tile_row(swa_q_g[l], N_HEADS) * (HEAD_DIM ** -0.5), 'gsk': tile_row(swa_k_g[l], SWA_KV_HEADS),
        'fb': fb,
        'gcq': mla_q_a_g[l][None, :], 'wqu': mla_w_q_up[l][:, qcols].astype(BF16),
        'gckv': mla_kv_a_g[l][None, :], 'wkvu': mla_w_kv_up[l][:, kcols].astype(BF16),
        'wvtf': w[:, o_fv:o_fv + 512].T.astype(BF16), 'wvtm': mla_w_kv_up[l][:, vcols].T.astype(BF16),
        'wvts': w[:, o_sv:o_sv + SWA_KV_HEADS * HEAD_DIM].T.astype(BF16),
        'gmq_n': tile_row(mla_q_g[l][:HEAD_DIM], N_HEADS) * (LOG2E * width ** -0.5),
        'gmq_r': tile_row(mla_q_g[l][HEAD_DIM:], LANES // ROPE_DIM) * (LOG2E * width ** -0.5),
        'gmk_n': tile_row(mla_k_g[l][:HEAD_DIM], N_HEADS),
        'gmk_r': tile_row(mla_k_g[l][HEAD_DIM:], LANES // ROPE_DIM),
        'sinks': swa_sinks[l].astype(F32),
        'wg': w[:, o_g:].astype(BF16), 'wb': wb, 'wo': w_o[l].astype(BF16),
        'g2': norm2_g[l][None, :], 'wup': ffn_w_up[l].astype(BF16), 'cw': ffn_conv_w[l], 'cb': ffn_conv_b[l][None, :],
        'wdn': ffn_w_down[l].astype(BF16),
    }


def kernel(x, meta_tokens, norm1_g, w_in, fox_forget_b, fox_q_g, fox_k_g, mla_q_a_g, mla_w_q_up, mla_kv_a_g, mla_w_kv_up, mla_q_g, mla_k_g, swa_q_g, swa_k_g, swa_sinks, w_branch, w_o, norm2_g, ffn_w_up, ffn_conv_w, ffn_conv_b, ffn_w_down):
    b, seq, d = x.shape
    assert seq % LANES == 0 and meta_tokens.shape == (N_META, d)
    lp = FRONT + seq
    tm = _row_tile(lp)
    tq = _row_tile(seq)
    consts = _constants(lp, tm)
    front = jnp.concatenate([jnp.zeros((PAD, d), x.dtype), meta_tokens.astype(x.dtype)], axis=0)
    h = jnp.concatenate([jnp.broadcast_to(front[None], (b, FRONT, d)), x], axis=1)
    weights = (norm1_g, w_in, fox_forget_b, fox_q_g, fox_k_g, mla_q_a_g, mla_w_q_up, mla_kv_a_g, mla_w_kv_up,
               mla_q_g, mla_k_g, swa_q_g, swa_k_g, swa_sinks, w_branch, w_o, norm2_g, ffn_w_up, ffn_conv_w,
               ffn_conv_b, ffn_w_down)
    for l in range(w_in.shape[0]):
        p = _layer_params(l, *weights)
        qf, kf, vtf, qm, km, vtm, sq, sk, vts = _in_proj(h, p, consts, tm)
        out_a = _causal_attn(qf, kf, vtf, consts['fox_masks'], tq)
        out_b = _causal_attn(qm, km, vtm, consts['mla_masks'], tq)
        out_c = _swa_attn(sq, sk, vts, p['sinks'], consts)
        h = _merge(h, out_a, out_b, out_c, p, tm)
        h = _ffn(h, p, tm)
    return h[:, FRONT:]
```

```python
import functools
import math

import numpy as np
import jax
import jax.numpy as jnp
from jax import lax
from jax.experimental import pallas as pl
from jax.experimental.pallas import tpu as pltpu

F32 = jnp.float32
BF16 = jnp.bfloat16

N_META = 16
EPS = 1e-6
N_HEADS = 8
HEAD_DIM = 64
ROPE_DIM = 32
MLA_Q_RANK = 256
MLA_KV_RANK = 128
SWA_KV_HEADS = 2
WINDOW = 128
ROPE_THETA = 10000.0
BRANCH_WIDTH = N_HEADS * HEAD_DIM
N_BRANCH = 3
CONV_WIDTH = 3

LANES = 128
FRONT = 128
PAD = FRONT - N_META
NEG = -1e30
DEC_PARTS = 3
LOG2E = math.log2(math.e)
ONES_ROWS = 16
VT_ROWS = HEAD_DIM + ONES_ROWS
QUERY_SUB = 256
CAUSAL_BLOCK = 1024
SWA_QUERY_BLOCK = 256
VMEM_LIMIT = 56 * 1024 * 1024

_A_FQ, _A_FK, _A_SQ, _A_SK, _A_CQ, _A_CKV, _A_KR, _A_FF = 0, 512, 1024, 1536, 1664, 1920, 2048, 2176
_NT = (((1,), (1,)), ((), ()))


def _row_tile(lp):
    best = 128
    for t in range(128, 513, 128):
        if lp % t == 0:
            best = t
    return best


def _whole(w, **kw):
    if isinstance(w, tuple):
        arr, layer = w
        return arr, pl.BlockSpec((None,) + arr.shape[1:], lambda *g: (layer,) + (0,) * (arr.ndim - 1), **kw)
    return w, pl.BlockSpec(w.shape, lambda *g: (0,) * w.ndim, **kw)


def _lane_iota(shape):
    return lax.broadcasted_iota(jnp.int32, shape, len(shape) - 1)


def _rms_rows(y, width):
    ss = jnp.sum(y * y, axis=-1, keepdims=True)
    return y * lax.rsqrt(ss * (1.0 / width) + EPS)


def _half_sums(y2):
    lane = _lane_iota(y2.shape)
    lo = jnp.sum(jnp.where(lane < HEAD_DIM, y2, 0.0), axis=-1, keepdims=True)
    hi = jnp.sum(jnp.where(lane >= HEAD_DIM, y2, 0.0), axis=-1, keepdims=True)
    return lo, hi


def _headnorm64_tile(yt):
    lo, hi = _half_sums(yt * yt)
    lane = _lane_iota(yt.shape)
    r = jnp.where(lane < HEAD_DIM, lax.rsqrt(lo * (1.0 / HEAD_DIM) + EPS), lax.rsqrt(hi * (1.0 / HEAD_DIM) + EPS))
    return yt * r


def _rotate_half(x):
    lane = _lane_iota(x.shape)
    half = ROPE_DIM // 2
    return jnp.where((lane % ROPE_DIM) < half, pltpu.roll(x, LANES - half, 1), pltpu.roll(x, half, 1))


def _quad_select(lane, vals):
    g = lane // ROPE_DIM
    out = jnp.where(g == 0, vals[0], vals[1])
    out = jnp.where(g == 2, vals[2], out)
    return jnp.where(g == 3, vals[3], out)


def _in_proj_kernel(h_ref, g1_ref, wa_ref, gfq_ref, gfk_ref, gsq_ref, gsk_ref, fb_ref,
                    gcq_ref, wqu_ref, gckv_ref, wkvu_ref, gmq_n_ref, gmq_r_ref, gmk_n_ref, gmk_r_ref,
                    wvtf_ref, wvtm_ref, wvts_ref,
                    cos_ref, sin_ref, tri_ref, eq_ref, ek_ref, qconst_ref, kconst_ref,
                    qf_ref, kf_ref, vtf_ref, qm_ref, km_ref, vtm_ref, sq_ref, sk_ref, vts_ref,
                    carry_ref, *, tm):
    j = pl.program_id(1)
    h = h_ref[...]
    xn = (_rms_rows(h, h.shape[-1]) * g1_ref[...]).astype(BF16)

    def proj(c0, width):
        return jnp.dot(xn, wa_ref[:, c0:c0 + width], preferred_element_type=F32)

    def store_vt(vt_ref, wt_ref, x):
        yt = lax.dot_general(wt_ref[...], x, _NT, preferred_element_type=F32)
        ones = jnp.ones((ONES_ROWS, tm), BF16)
        for hd in range(yt.shape[0] // HEAD_DIM):
            vt_ref[hd * VT_ROWS:hd * VT_ROWS + HEAD_DIM, :] = yt[hd * HEAD_DIM:(hd + 1) * HEAD_DIM, :].astype(BF16)
            vt_ref[hd * VT_ROWS + HEAD_DIM:(hd + 1) * VT_ROWS, :] = ones

    yq = proj(_A_FQ, 512)
    yk = proj(_A_FK, 512)
    for t in range(4):
        sl = slice(t * LANES, (t + 1) * LANES)
        qf_ref[:, 2 * t * LANES:(2 * t + 1) * LANES] = (_headnorm64_tile(yq[:, sl]) * gfq_ref[:, sl]).astype(BF16)
        kf_ref[:, 2 * t * LANES:(2 * t + 1) * LANES] = (_headnorm64_tile(yk[:, sl]) * gfk_ref[:, sl]).astype(BF16)
    store_vt(vtf_ref, wvtf_ref, xn)

    @pl.when(j == 0)
    def _():
        carry_ref[...] = jnp.zeros_like(carry_ref)

    z = proj(_A_FF, LANES) + fb_ref[...]
    ls = jnp.minimum(z, 0.0) - jnp.log1p(jnp.exp(-jnp.abs(z)))
    row = j * tm + lax.broadcasted_iota(jnp.int32, ls.shape, 0)
    ls = jnp.where((_lane_iota(ls.shape) < N_HEADS) & (row >= PAD), ls * LOG2E, 0.0)
    tri = tri_ref[...]
    c = carry_ref[0:1, :]
    rem = ls
    for _ in range(DEC_PARTS):
        part = rem.astype(BF16)
        c = c + jnp.dot(tri, part, preferred_element_type=F32)
        rem = rem - part.astype(F32)
    carry_ref[0:1, :] = c[tm - 1:tm, :]
    qd = qconst_ref[...]
    kd = kconst_ref[...]
    rem = c
    for i in range(DEC_PARTS):
        part = rem.astype(BF16)
        qd = qd + jnp.dot(part, eq_ref[i], preferred_element_type=F32)
        kd = kd + jnp.dot(part, ek_ref[i], preferred_element_type=F32)
        rem = rem - part.astype(F32)
    qd = qd.astype(BF16)
    kd = kd.astype(BF16)
    for t in range(4):
        qf_ref[:, (2 * t + 1) * LANES:(2 * t + 2) * LANES] = qd
        kf_ref[:, (2 * t + 1) * LANES:(2 * t + 2) * LANES] = kd

    ysq = proj(_A_SQ, 512)
    for t in range(4):
        sl = slice(t * LANES, (t + 1) * LANES)
        sq_ref[:, sl] = (_headnorm64_tile(ysq[:, sl]) * gsq_ref[:, sl]).astype(BF16)
    sk_ref[...] = (_headnorm64_tile(proj(_A_SK, LANES)) * gsk_ref[...]).astype(BF16)
    store_vt(vts_ref, wvts_ref, xn)

    cos = cos_ref[...]
    sin = sin_ref[...]

    def rope(x):
        return x * cos + _rotate_half(x) * sin

    cq = (_rms_rows(proj(_A_CQ, MLA_Q_RANK), MLA_Q_RANK) * gcq_ref[...]).astype(BF16)
    yq = jnp.dot(cq, wqu_ref[...], preferred_element_type=F32)
    lane = _lane_iota((tm, LANES))
    quad_id = lane // ROPE_DIM
    ss = []
    for t in range(4):
        lo, hi = _half_sums(jnp.square(yq[:, t * LANES:(t + 1) * LANES]))
        ss += [lo, hi]
    for u in range(2):
        y2 = jnp.square(yq[:, 512 + u * LANES:512 + (u + 1) * LANES])
        for g in range(4):
            ss[4 * u + g] = ss[4 * u + g] + jnp.sum(jnp.where(quad_id == g, y2, 0.0), axis=-1, keepdims=True)
    width = HEAD_DIM + ROPE_DIM
    rq = [lax.rsqrt(s * (1.0 / width) + EPS) for s in ss]
    q_rope = []
    for u in range(2):
        sl = slice(512 + u * LANES, 512 + (u + 1) * LANES)
        x = yq[:, sl] * _quad_select(lane, rq[4 * u:4 * u + 4]) * gmq_r_ref[...]
        q_rope.append(rope(x).astype(BF16))
    for t in range(4):
        sl = slice(t * LANES, (t + 1) * LANES)
        r = jnp.where(lane < HEAD_DIM, rq[2 * t], rq[2 * t + 1])
        qm_ref[:, 2 * t * LANES:(2 * t + 1) * LANES] = (yq[:, sl] * r * gmq_n_ref[:, sl]).astype(BF16)
        qm_ref[:, (2 * t + 1) * LANES:(2 * t + 2) * LANES] = q_rope[t // 2]

    ckv = (_rms_rows(proj(_A_CKV, MLA_KV_RANK), MLA_KV_RANK) * gckv_ref[...]).astype(BF16)
    ykv = jnp.dot(ckv, wkvu_ref[...], preferred_element_type=F32)
    kr4 = proj(_A_KR, LANES)
    ss_rope = jnp.sum(kr4 * kr4, axis=-1, keepdims=True) * 0.25
    rk = []
    for t in range(4):
        lo, hi = _half_sums(jnp.square(ykv[:, t * LANES:(t + 1) * LANES]))
        rk += [lax.rsqrt((lo + ss_rope) * (1.0 / width) + EPS), lax.rsqrt((hi + ss_rope) * (1.0 / width) + EPS)]
    kr_base = rope(kr4 * gmk_r_ref[...])
    k_rope = [(kr_base * _quad_select(lane, rk[4 * u:4 * u + 4])).astype(BF16) for u in range(2)]
    for t in range(4):
        sl = slice(t * LANES, (t + 1) * LANES)
        r = jnp.where(lane < HEAD_DIM, rk[2 * t], rk[2 * t + 1])
        km_ref[:, 2 * t * LANES:(2 * t + 1) * LANES] = (ykv[:, sl] * r * gmk_n_ref[:, sl]).astype(BF16)
        km_ref[:, (2 * t + 1) * LANES:(2 * t + 2) * LANES] = k_rope[t // 2]
    store_vt(vtm_ref, wvtm_ref, ckv)


def _in_proj(h, p, consts, tm):
    b, lp, d = h.shape
    nt = lp // tm
    row_blk = lambda w: pl.BlockSpec((None, tm, w), lambda bi, j: (bi, j, 0))
    full = lambda a: pl.BlockSpec(a.shape, lambda bi, j: (0,) * a.ndim)
    tab = pl.BlockSpec((tm, LANES), lambda bi, j: (j, 0))
    ins = [h, p['g1'], p['wa'], p['gfq'], p['gfk'], p['gsq'], p['gsk'], p['fb'],
           p['gcq'], p['wqu'], p['gckv'], p['wkvu'], p['gmq_n'], p['gmq_r'], p['gmk_n'], p['gmk_r'],
           p['wvtf'], p['wvtm'], p['wvts']]
    whole = [_whole(a) for a in ins[1:]]
    ins = [h] + [a for a, _ in whole]
    in_specs = [row_blk(d)] + [spec for _, spec in whole]
    ins += [consts['cos'], consts['sin'], consts['tri'], consts['eq'], consts['ek'], consts['qconst'], consts['kconst']]
    in_specs += [tab, tab] + [full(consts[k]) for k in ('tri', 'eq', 'ek', 'qconst', 'kconst')]
    outs = [('row', 1024), ('row', 1024), ('vt', N_HEADS), ('row', 1024), ('row', 1024), ('vt', N_HEADS),
            ('row', 512), ('row', 128), ('vt', SWA_KV_HEADS)]
    out_shape = [jax.ShapeDtypeStruct((b, lp, w) if kind == 'row' else (b, w * VT_ROWS, lp), BF16) for kind, w in outs]
    out_specs = [row_blk(w) if kind == 'row' else pl.BlockSpec((None, w * VT_ROWS, tm), lambda bi, j: (bi, 0, j))
                 for kind, w in outs]
    return pl.pallas_call(
        functools.partial(_in_proj_kernel, tm=tm),
        grid=(b, nt),
        in_specs=in_specs,
        out_specs=out_specs,
        out_shape=out_shape,
        scratch_shapes=[pltpu.VMEM((8, LANES), F32)],
        compiler_params=pltpu.CompilerParams(dimension_semantics=("arbitrary", "arbitrary"),
                                             vmem_limit_bytes=VMEM_LIMIT),
        name="in_proj",
    )(*ins)


def _causal_attn_kernel(mask_ref, q_ref, k_ref, vt_ref, o_ref, m_ref, acc_ref, qs_ref, s_ref, *sd_refs, tq, n_blocks):
    masks = mask_ref[...]

    def init(n):
        m_ref[:, :, 0:n] = jnp.full((2, 1, n), NEG, F32)
        acc_ref[:, :, 0:n] = jnp.zeros((2, VT_ROWS, n), F32)

    def update(unit, st, vt, visible):
        a, c0, nc = unit
        if visible is not None:
            st = jnp.where(visible, st, NEG)
        cols = slice(c0, c0 + nc)
        m_prev = m_ref[a, :, cols]
        m_new = jnp.maximum(m_prev, jnp.max(st, axis=0, keepdims=True))
        alpha = jnp.exp2(m_prev - m_new)
        p = jnp.exp2(st - m_new).astype(BF16)
        acc_ref[a, :, cols] = alpha * acc_ref[a, :, cols] + jnp.dot(vt, p, preferred_element_type=F32)
        m_ref[a, :, cols] = m_new

    def finish(r0, n):
        ot = jnp.concatenate([acc_ref[a, 0:HEAD_DIM, 0:n] * (1.0 / acc_ref[a, HEAD_DIM:HEAD_DIM + 1, 0:n])
                              for a in range(2)], axis=0)
        o_ref[pl.ds(r0, n), :] = ot.T.astype(o_ref.dtype)

    def set_queries(qslot, r0, n):
        q = q_ref[pl.ds(r0, n), :]
        for a in range(2):
            qs_ref[qslot, a, 0:n, :] = q * masks[a:a + 1, :]

    def scores(qslot, unit, keys):
        a, c0, nc = unit
        return lax.dot_general(keys, qs_ref[qslot, a, c0:c0 + nc, :], _NT, preferred_element_type=F32)

    def values(a, k0, nk):
        return vt_ref[a * VT_ROWS:(a + 1) * VT_ROWS, pl.ds(k0, nk)]

    def visible(shape, c0, causal_lead):
        row = lax.broadcasted_iota(jnp.int32, shape, 0)
        col = lax.broadcasted_iota(jnp.int32, shape, 1)
        lead = row < FRONT
        lowest = jnp.where(lead, PAD, FRONT)
        highest = jnp.where(lead, col if causal_lead else FRONT, col + (c0 + FRONT))
        return (row >= lowest) & (row <= highest)

    init(FRONT)
    set_queries(0, 0, FRONT)
    lead_units = [(0, 0, FRONT), (1, 0, FRONT)]
    lead_keys = k_ref[0:FRONT, :]
    lead_sts = [scores(0, u, lead_keys) for u in lead_units]
    for u, st in zip(lead_units, lead_sts):
        update(u, st, values(u[0], 0, FRONT), visible(st.shape, 0, True))
    finish(0, FRONT)

    n_sub = tq // QUERY_SUB
    units = [(a, s * QUERY_SUB, QUERY_SUB) for a in range(2) for s in range(n_sub)]
    diag_keys = [FRONT + unit[1] + QUERY_SUB for unit in units]

    def block_start(j):
        return pl.multiple_of(FRONT + j * tq, LANES)

    def park_block(slot, qslot, j):
        keys = k_ref[pl.ds(block_start(j), tq), :]
        for u, unit in enumerate(units):
            s_ref[slot, u] = scores(qslot, unit, keys)

    def consume_block(slot, j):
        for u, unit in enumerate(units):
            update(unit, s_ref[slot, u], values(unit[0], block_start(j), tq), None)

    def park_diag(qslot, i):
        kk = jnp.concatenate([k_ref[0:FRONT, :], k_ref[pl.ds(block_start(i), tq), :]], axis=0)
        for u, (unit, nk) in enumerate(zip(units, diag_keys)):
            sd_refs[u][...] = scores(qslot, unit, kk[0:nk, :])

    def consume_diag(i):
        vts = [jnp.concatenate([values(a, 0, FRONT), values(a, block_start(i), tq)], axis=1) for a in range(2)]
        for u, (unit, nk) in enumerate(zip(units, diag_keys)):
            update(unit, sd_refs[u][...], vts[unit[0]][:, 0:nk], visible((nk, unit[2]), unit[1], False))

    def visible_pairs(qslot, i, n_pairs):
        def pair(t, c):
            park_block(1, qslot, 2 * t + 1)
            consume_block(0, 2 * t)
            park_block(0, qslot, jnp.minimum(2 * t + 2, i - 1))
            consume_block(1, 2 * t + 1)
            return c
        lax.fori_loop(0, n_pairs, pair, 0)

    set_queries(0, block_start(0), tq)

    def two_blocks(g, carry):
        even, odd = 2 * g, 2 * g + 1
        init(tq)
        visible_pairs(0, even, g)
        set_queries(1, block_start(odd), tq)
        park_diag(0, even)
        park_block(0, 1, 0)
        consume_diag(even)
        finish(block_start(even), tq)
        init(tq)
        visible_pairs(1, odd, g)
        park_diag(1, odd)
        consume_block(0, 2 * g)
        nxt = jnp.minimum(even + 2, n_blocks - 1)
        set_queries(0, block_start(nxt), tq)
        park_block(0, 0, 0)
        consume_diag(odd)
        finish(block_start(odd), tq)
        return carry

    lax.fori_loop(0, n_blocks // 2, two_blocks, 0)


def _causal_attn(q, k, vt, masks, tq):
    b, _, lp = vt.shape
    n_pairs = N_HEADS // 2
    n_blocks = (lp - FRONT) // tq
    assert tq % QUERY_SUB == 0 and n_blocks % 2 == 0
    n_sub = tq // QUERY_SUB
    n_units = 2 * n_sub

    def qk_spec():
        return pl.BlockSpec((None, lp, 2 * LANES), lambda bi, p: (bi, 0, p))

    return pl.pallas_call(
        functools.partial(_causal_attn_kernel, tq=tq, n_blocks=n_blocks),
        grid=(b, n_pairs),
        in_specs=[pl.BlockSpec((None, 2, 2 * LANES), lambda bi, p: (p, 0, 0)),
                  qk_spec(), qk_spec(),
                  pl.BlockSpec((None, 2 * VT_ROWS, lp), lambda bi, p: (bi, p, 0))],
        out_specs=pl.BlockSpec((None, lp, LANES), lambda bi, p: (bi, 0, p)),
        out_shape=jax.ShapeDtypeStruct((b, lp, BRANCH_WIDTH), BF16),
        scratch_shapes=[pltpu.VMEM((2, 1, tq), F32), pltpu.VMEM((2, VT_ROWS, tq), F32),
                        pltpu.VMEM((2, 2, tq, 2 * LANES), BF16),
                        pltpu.VMEM((2, n_units, tq, QUERY_SUB), F32),
                        *[pltpu.VMEM((FRONT + (u % n_sub + 1) * QUERY_SUB, QUERY_SUB), F32) for u in range(n_units)]],
        compiler_params=pltpu.CompilerParams(dimension_semantics=("arbitrary", "arbitrary"),
                                             vmem_limit_bytes=VMEM_LIMIT),
        name="causal_attn",
    )(masks, q, k, vt)


def _swa_kernel(sinks_ref, mask_ref, q_ref, qaux_ref, k_ref, kaux_ref, vt_ref, o_ref, bias_ref, s_ref, *, qb, n_blocks):
    masks = mask_ref[...]
    n_band = WINDOW + qb

    def key_bias(shape, n_band_rows, band_shift):
        row = lax.broadcasted_iota(jnp.int32, shape, 0)
        col = lax.broadcasted_iota(jnp.int32, shape, 1)
        band = row < n_band_rows
        lowest = jnp.where(band, col + (band_shift + 1), 0)
        highest = jnp.where(band, col + (band_shift + WINDOW), n_band_rows + N_META)
        return jnp.where((row >= lowest) & (row <= highest), 0.0, NEG)

    def keys(key_slices):
        return jnp.concatenate([jnp.concatenate([k_ref[pl.ds(s, n), :], kaux_ref[pl.ds(s, n), :]], axis=1)
                                for s, n in key_slices], axis=0)

    def values(key_slices):
        return [jnp.concatenate([vt_ref[g * VT_ROWS:(g + 1) * VT_ROWS, pl.ds(s, n)] for s, n in key_slices], axis=1)
                for g in range(SWA_KV_HEADS)]

    def tile_scores(r0, nq, t, kk):
        cols = slice(t * LANES, (t + 1) * LANES)
        qt = jnp.concatenate([q_ref[pl.ds(r0, nq), cols], qaux_ref[pl.ds(r0, nq), cols]], axis=1)
        return [lax.dot_general(kk, qt * masks[a:a + 1, :], _NT, preferred_element_type=F32) for a in range(2)]

    def tile_finish(r0, nq, t, sts, vts, bias):
        outs = []
        for a in range(2):
            sink = sinks_ref[t + 4 * a]
            st = sts[a] + bias
            m = jnp.maximum(jnp.max(st, axis=0, keepdims=True), sink)
            p = jnp.exp(st - m).astype(BF16)
            acc = jnp.dot(vts[a], p, preferred_element_type=F32)
            den = acc[HEAD_DIM:HEAD_DIM + 1, :] + jnp.exp(sink - m)
            outs.append(acc[0:HEAD_DIM, :] * (1.0 / den))
        o_ref[pl.ds(r0, nq), t * LANES:(t + 1) * LANES] = jnp.concatenate(outs, axis=0).T.astype(o_ref.dtype)

    def attend(r0, nq, key_slices, bias):
        kk, vts = keys(key_slices), values(key_slices)
        for t in range(4):
            tile_finish(r0, nq, t, tile_scores(r0, nq, t, kk), vts, bias)

    row = lax.broadcasted_iota(jnp.int32, (FRONT, FRONT), 0)
    col = lax.broadcasted_iota(jnp.int32, (FRONT, FRONT), 1)
    attend(0, FRONT, [(0, FRONT)], jnp.where((row >= PAD) & (row <= col), 0.0, NEG))
    meta = (PAD, N_META)
    attend(FRONT, qb, [(FRONT, qb), meta], key_bias((qb + N_META, qb), qb, -WINDOW))
    bias_ref[...] = key_bias(bias_ref.shape, n_band, 0)

    def block_keys(i):
        return [(pl.multiple_of(FRONT + i * qb - WINDOW, LANES), n_band), meta]

    def park_scores(slot, i, t):
        sts = tile_scores(pl.multiple_of(FRONT + i * qb, LANES), qb, t, keys(block_keys(i)))
        for a in range(2):
            s_ref[slot, a] = sts[a]

    park_scores(0, 1, 0)

    def q_block(i, carry):
        r0 = pl.multiple_of(FRONT + i * qb, LANES)
        vts = values(block_keys(i))
        for t in range(4):
            if t < 3:
                park_scores((t + 1) % 2, i, t + 1)
            else:
                park_scores(0, jnp.minimum(i + 1, n_blocks - 1), 0)
            tile_finish(r0, qb, t, [s_ref[t % 2, a] for a in range(2)], vts, bias_ref[...])
        return carry

    lax.fori_loop(1, n_blocks, q_block, 0)


def _swa_attn(q, k, vt, sinks, consts):
    b, lp, _ = q.shape
    qb = SWA_QUERY_BLOCK
    assert (lp - FRONT) % qb == 0
    full = lambda a: pl.BlockSpec(a.shape, lambda bi: (0,) * a.ndim)
    per_batch = lambda r, c: pl.BlockSpec((None, r, c), lambda bi: (bi, 0, 0))
    return pl.pallas_call(
        functools.partial(_swa_kernel, qb=qb, n_blocks=(lp - FRONT) // qb),
        grid=(b,),
        in_specs=[pl.BlockSpec(memory_space=pltpu.SMEM), full(consts['swa_masks']),
                  per_batch(lp, BRANCH_WIDTH), full(consts['swa_qaux']),
                  per_batch(lp, LANES), full(consts['swa_kaux']),
                  per_batch(SWA_KV_HEADS * VT_ROWS, lp)],
        out_specs=per_batch(lp, BRANCH_WIDTH),
        out_shape=jax.ShapeDtypeStruct((b, lp, BRANCH_WIDTH), BF16),
        scratch_shapes=[pltpu.VMEM((WINDOW + qb + N_META, qb), F32),
                        pltpu.VMEM((2, 2, WINDOW + qb + N_META, qb), F32)],
        compiler_params=pltpu.CompilerParams(dimension_semantics=("arbitrary",), vmem_limit_bytes=VMEM_LIMIT),
        name="swa_attn",
    )(sinks, consts['swa_masks'], q, consts['swa_qaux'], k, consts['swa_kaux'], vt)


def _merge_kernel(h_ref, oa_ref, ob_ref, oc_ref, g1_ref, wg_ref, wb_ref, wo_ref, out_ref):
    h = h_ref[...]
    d = h.shape[-1]
    xn = (_rms_rows(h, d) * g1_ref[...]).astype(BF16)
    merged = None
    for n, o_ref in enumerate((oa_ref, ob_ref, oc_ref)):
        gate = jax.nn.sigmoid(jnp.dot(xn, wg_ref[:, n * d:(n + 1) * d], preferred_element_type=F32))
        y = jnp.dot(o_ref[...], wb_ref[n], preferred_element_type=F32)
        merged = gate * y if merged is None else merged + gate * y
    out_ref[...] = h + jnp.dot(merged.astype(BF16), wo_ref[...], preferred_element_type=F32)


def _merge(h, oa, ob, oc, p, tm):
    b, lp, d = h.shape
    row_blk = lambda w: pl.BlockSpec((None, tm, w), lambda bi, j: (bi, j, 0))
    whole = [_whole(p[k], pipeline_mode=pl.Buffered(1)) for k in ('g1', 'wg', 'wb', 'wo')]
    ws = [a for a, _ in whole]
    return pl.pallas_call(
        _merge_kernel,
        grid=(b, lp // tm),
        in_specs=[row_blk(d), row_blk(BRANCH_WIDTH), row_blk(BRANCH_WIDTH), row_blk(BRANCH_WIDTH)]
        + [spec for _, spec in whole],
        out_specs=row_blk(d),
        out_shape=jax.ShapeDtypeStruct(h.shape, F32),
        compiler_params=pltpu.CompilerParams(dimension_semantics=("arbitrary", "arbitrary"),
                                             vmem_limit_bytes=VMEM_LIMIT),
        name="merge",
    )(h, oa, ob, oc, *ws)


_FF_CHUNK = 256
_HALO = 8


def _ffn_kernel(h_ref, g2_ref, wup_ref, cw_ref, cb_ref, wdn_ref, out_ref, carry_ref, ubuf_ref, act_ref, *, tm, d_ff):
    j = pl.program_id(1)
    h = h_ref[...]
    xn = _rms_rows(h, h.shape[-1]) * g2_ref[...]
    row = j * tm + lax.broadcasted_iota(jnp.int32, (tm, 1), 0)
    xn = jnp.where(row >= PAD, xn, 0.0).astype(BF16)

    @pl.when(j == 0)
    def _():
        carry_ref[...] = jnp.zeros_like(carry_ref)

    for c in range(d_ff // _FF_CHUNK):
        acts = []
        for half in range(2):
            c0 = half * d_ff + c * _FF_CHUNK
            cols = slice(c0, c0 + _FF_CHUNK)
            u = jnp.dot(xn, wup_ref[:, cols], preferred_element_type=F32)
            ubuf_ref[half, 0:_HALO, :] = carry_ref[:, cols]
            ubuf_ref[half, _HALO:_HALO + tm, :] = u
            carry_ref[:, cols] = u[tm - _HALO:tm, :]
            u1 = ubuf_ref[half, _HALO - 1:_HALO - 1 + tm, :]
            u2 = ubuf_ref[half, _HALO - 2:_HALO - 2 + tm, :]
            acts.append(cb_ref[:, cols] + cw_ref[2:3, cols] * u + cw_ref[1:2, cols] * u1 + cw_ref[0:1, cols] * u2)
        gate, val = acts
        act_ref[:, c * _FF_CHUNK:(c + 1) * _FF_CHUNK] = (gate * jax.nn.sigmoid(gate) * val).astype(BF16)
    out_ref[...] = h + jnp.dot(act_ref[...], wdn_ref[...], preferred_element_type=F32)


def _ffn(h, p, tm):
    b, lp, d = h.shape
    row_blk = pl.BlockSpec((None, tm, d), lambda bi, j: (bi, j, 0))
    whole = [_whole(p[k], pipeline_mode=pl.Buffered(1)) for k in ('g2', 'wup', 'cw', 'cb', 'wdn')]
    ws = [a for a, _ in whole]
    d_ff = ws[-1].shape[-2]
    return pl.pallas_call(
        functools.partial(_ffn_kernel, tm=tm, d_ff=d_ff),
        grid=(b, lp // tm),
        in_specs=[row_blk] + [spec for _, spec in whole],
        out_specs=row_blk,
        out_shape=jax.ShapeDtypeStruct(h.shape, F32),
        scratch_shapes=[pltpu.VMEM((_HALO, 2 * d_ff), F32),
                        pltpu.VMEM((2, _HALO + tm, _FF_CHUNK), F32),
                        pltpu.VMEM((tm, d_ff), BF16)],
        compiler_params=pltpu.CompilerParams(dimension_semantics=("arbitrary", "arbitrary"),
                                             vmem_limit_bytes=VMEM_LIMIT),
        name="conv_ffn",
    )(h, *ws)


def _constants(lp, tm):
    half = ROPE_DIM // 2
    freqs = ROPE_THETA ** (-np.arange(half, dtype=np.float64) / half)
    lane = np.arange(LANES)
    ang = (np.arange(lp) - PAD).astype(np.float64)[:, None] * freqs[lane % half][None, :]
    sign = np.where((lane % ROPE_DIM) < half, -1.0, 1.0)
    row = np.arange(lp)
    pos_hi, pos_lo = row // LANES, row % LANES
    swa_kaux = np.zeros((lp, LANES), np.float32)
    swa_qaux = np.zeros((lp, BRANCH_WIDTH), np.float32)
    swa_masks = np.zeros((2, 2 * LANES), np.float32)
    for a in range(2):
        swa_masks[a, a * HEAD_DIM:(a + 1) * HEAD_DIM] = 1.0
        swa_masks[a, LANES + 4 * a:LANES + 4 * (a + 1)] = 1.0
        swa_kaux[:, 4 * a:4 * (a + 1)] = np.stack([np.ones(lp), np.ones(lp), pos_hi, pos_lo], axis=1)
        for t in range(4):
            slope = 2.0 ** (-8.0 * (t + 4 * a + 1) / N_HEADS)
            swa_qaux[:, t * LANES + 4 * a:t * LANES + 4 * (a + 1)] = np.stack(
                [-LANES * slope * pos_hi, -slope * pos_lo, np.full(lp, LANES * slope), np.full(lp, slope)], axis=1)
    tri = np.tril(np.ones((tm, tm), np.float32))
    eq = np.zeros((DEC_PARTS, LANES, LANES), np.float32)
    ek = np.zeros((DEC_PARTS, LANES, LANES), np.float32)
    qconst = np.zeros((1, LANES), np.float32)
    kconst = np.zeros((1, LANES), np.float32)
    for hd in range(N_HEADS):
        base = 2 * DEC_PARTS * hd
        for i in range(DEC_PARTS):
            eq[i, hd, base + i] = 1.0
            ek[i, hd, base + DEC_PARTS + i] = -1.0
            qconst[0, base + DEC_PARTS + i] = 1.0
            kconst[0, base + i] = 1.0
    fox_masks = np.zeros((N_HEADS // 2, 2, 2 * LANES), np.float32)
    mla_masks = np.zeros((N_HEADS // 2, 2, 2 * LANES), np.float32)
    for p in range(N_HEADS // 2):
        for a in range(2):
            hd = 2 * p + a
            fox_masks[p, a, a * HEAD_DIM:(a + 1) * HEAD_DIM] = 1.0
            fox_masks[p, a, LANES + 2 * DEC_PARTS * hd:LANES + 2 * DEC_PARTS * (hd + 1)] = 1.0
            mla_masks[p, a, a * HEAD_DIM:(a + 1) * HEAD_DIM] = 1.0
            g = hd % 4
            mla_masks[p, a, LANES + g * ROPE_DIM:LANES + (g + 1) * ROPE_DIM] = 1.0
    return {
        'cos': jnp.asarray(np.cos(ang), F32), 'sin': jnp.asarray(np.sin(ang) * sign[None, :], F32),
        'tri': jnp.asarray(tri, BF16), 'eq': jnp.asarray(eq, BF16), 'ek': jnp.asarray(ek, BF16),
        'qconst': jnp.asarray(qconst), 'kconst': jnp.asarray(kconst),
        'fox_masks': jnp.asarray(fox_masks, BF16), 'mla_masks': jnp.asarray(mla_masks, BF16),
        'swa_masks': jnp.asarray(swa_masks, BF16), 'swa_qaux': jnp.asarray(swa_qaux, BF16),
        'swa_kaux': jnp.asarray(swa_kaux, BF16),
    }


def _swa_head_perm():
    cols = []
    for t in range(4):
        for hd in (t, 4 + t):
            cols += list(range(hd * HEAD_DIM, (hd + 1) * HEAD_DIM))
    return np.asarray(cols)


_O_FQ, _O_FK, _O_FV, _O_FF = 0, 512, 1024, 1536
_O_CQ = _O_FF + N_HEADS
_O_CKV = _O_CQ + MLA_Q_RANK
_O_KR = _O_CKV + MLA_KV_RANK
_O_SQ = _O_KR + ROPE_DIM
_O_SK = _O_SQ + 512
_O_SV = _O_SK + SWA_KV_HEADS * HEAD_DIM
_O_GATES = _O_SV + SWA_KV_HEADS * HEAD_DIM


def _stacked_weights(w_in, w_branch, w_o, ffn_w_up, ffn_w_down):
    n_layers, d, _ = w_in.shape
    perm = _swa_head_perm()
    col = lambda c0, n: w_in[:, :, c0:c0 + n]
    wa = jnp.concatenate([
        col(_O_FQ, 512), col(_O_FK, 512), col(_O_SQ, 512)[:, :, perm], col(_O_SK, 128),
        col(_O_CQ, MLA_Q_RANK), col(_O_CKV, MLA_KV_RANK),
        jnp.tile(col(_O_KR, ROPE_DIM), (1, 1, LANES // ROPE_DIM)),
        col(_O_FF, N_HEADS), jnp.zeros((n_layers, d, LANES - N_HEADS), w_in.dtype)], axis=2).astype(BF16)
    wb = jnp.concatenate([w_branch[:, :2], w_branch[:, 2:3][:, :, perm, :]], axis=1).astype(BF16)
    return {'wa': wa, 'wg': w_in[:, :, _O_GATES:].astype(BF16), 'wb': wb, 'wo': w_o.astype(BF16),
            'wup': ffn_w_up.astype(BF16), 'wdn': ffn_w_down.astype(BF16)}


def _layer_params(l, stacked, norm1_g, w_in, fox_forget_b, fox_q_g, fox_k_g, mla_q_a_g, mla_w_q_up, mla_kv_a_g,
                  mla_w_kv_up, mla_q_g, mla_k_g, swa_q_g, swa_k_g, swa_sinks, w_branch, w_o,
                  norm2_g, ffn_w_up, ffn_conv_w, ffn_conv_b, ffn_w_down):
    w = w_in[l]
    tile_row = lambda g, n: jnp.tile(g, n)[None, :].astype(F32)
    width = HEAD_DIM + ROPE_DIM
    qcols = np.concatenate([np.concatenate([np.arange(hd * width, hd * width + HEAD_DIM) for hd in range(N_HEADS)]),
                            np.concatenate([np.arange(hd * width + HEAD_DIM, (hd + 1) * width) for hd in range(N_HEADS)])])
    kcols = np.concatenate([np.arange(hd * 2 * HEAD_DIM, hd * 2 * HEAD_DIM + HEAD_DIM) for hd in range(N_HEADS)])
    vcols = kcols + HEAD_DIM
    fb = jnp.concatenate([fox_forget_b[l], jnp.zeros((LANES - N_HEADS,), F32)])[None, :]
    return {
        'g1': norm1_g[l][None, :], 'wa': (stacked['wa'], l),
        'gfq': tile_row(fox_q_g[l], N_HEADS) * (LOG2E * HEAD_DIM ** -0.5), 'gfk': tile_row(fox_k_g[l], N_HEADS),
        'gsq': tile_row(swa_q_g[l], N_HEADS) * (HEAD_DIM ** -0.5), 'gsk': tile_row(swa_k_g[l], SWA_KV_HEADS),
        'fb': fb,
        'gcq': mla_q_a_g[l][None, :], 'wqu': mla_w_q_up[l][:, qcols].astype(BF16),
        'gckv': mla_kv_a_g[l][None, :], 'wkvu': mla_w_kv_up[l][:, kcols].astype(BF16),
        'wvtf': w[:, _O_FV:_O_FV + 512].T.astype(BF16), 'wvtm': mla_w_kv_up[l][:, vcols].T.astype(BF16),
        'wvts': w[:, _O_SV:_O_SV + SWA_KV_HEADS * HEAD_DIM].T.astype(BF16),
        'gmq_n': tile_row(mla_q_g[l][:HEAD_DIM], N_HEADS) * (LOG2E * width ** -0.5),
        'gmq_r': tile_row(mla_q_g[l][HEAD_DIM:], LANES // ROPE_DIM) * (LOG2E * width ** -0.5),
        'gmk_n': tile_row(mla_k_g[l][:HEAD_DIM], N_HEADS),
        'gmk_r': tile_row(mla_k_g[l][HEAD_DIM:], LANES // ROPE_DIM),
        'sinks': swa_sinks[l].astype(F32),
        'wg': (stacked['wg'], l), 'wb': (stacked['wb'], l), 'wo': (stacked['wo'], l),
        'g2': norm2_g[l][None, :], 'wup': (stacked['wup'], l), 'cw': ffn_conv_w[l], 'cb': ffn_conv_b[l][None, :],
        'wdn': (stacked['wdn'], l),
    }


def kernel(x, meta_tokens, norm1_g, w_in, fox_forget_b, fox_q_g, fox_k_g, mla_q_a_g, mla_w_q_up, mla_kv_a_g, mla_w_kv_up, mla_q_g, mla_k_g, swa_q_g, swa_k_g, swa_sinks, w_branch, w_o, norm2_g, ffn_w_up, ffn_conv_w, ffn_conv_b, ffn_w_down):
    b, seq, d = x.shape
    assert seq % LANES == 0 and meta_tokens.shape == (N_META, d)
    lp = FRONT + seq
    tm = _row_tile(lp)
    tq = CAUSAL_BLOCK if seq % (2 * CAUSAL_BLOCK) == 0 else _row_tile(seq)
    consts = _constants(lp, tm)
    front = jnp.concatenate([jnp.zeros((PAD, d), x.dtype), meta_tokens.astype(x.dtype)], axis=0)
    h = jnp.concatenate([jnp.broadcast_to(front[None], (b, FRONT, d)), x], axis=1)
    weights = (norm1_g, w_in, fox_forget_b, fox_q_g, fox_k_g, mla_q_a_g, mla_w_q_up, mla_kv_a_g, mla_w_kv_up,
               mla_q_g, mla_k_g, swa_q_g, swa_k_g, swa_sinks, w_branch, w_o, norm2_g, ffn_w_up, ffn_conv_w,
               ffn_conv_b, ffn_w_down)
    stacked = _stacked_weights(w_in, w_branch, w_o, ffn_w_up, ffn_w_down)
    for l in range(w_in.shape[0]):
        p = _layer_params(l, stacked, *weights)
        qf, kf, vtf, qm, km, vtm, sq, sk, vts = _in_proj(h, p, consts, tm)
        out_a = _causal_attn(qf, kf, vtf, consts['fox_masks'], tq)
        out_b = _causal_attn(qm, km, vtm, consts['mla_masks'], tq)
        out_c = _swa_attn(sq, sk, vts, p['sinks'], consts)
        h = _merge(h, out_a, out_b, out_c, p, tm)
        h = _ffn(h, p, tm)
    return h[:, FRONT:]
```

```python
import functools
import math

import numpy as np
import jax
import jax.numpy as jnp
from jax import lax
from jax.experimental import pallas as pl
from jax.experimental.pallas import tpu as pltpu

F32 = jnp.float32
BF16 = jnp.bfloat16

N_META = 16
EPS = 1e-6
N_HEADS = 8
HEAD_DIM = 64
ROPE_DIM = 32
MLA_Q_RANK = 256
MLA_KV_RANK = 128
SWA_KV_HEADS = 2
WINDOW = 128
ROPE_THETA = 10000.0
BRANCH_WIDTH = N_HEADS * HEAD_DIM
N_BRANCH = 3
CONV_WIDTH = 3

LANES = 128
FRONT = 128
PAD = FRONT - N_META
NEG = -1e30
DEC_PARTS = 3
LOG2E = math.log2(math.e)
ONES_ROWS = 16
VT_ROWS = HEAD_DIM + ONES_ROWS
QUERY_SUB = 256
CAUSAL_BLOCK = 1024
SWA_QUERY_BLOCK = 256
VMEM_LIMIT = 56 * 1024 * 1024

_A_FQ, _A_FK, _A_SQ, _A_SK, _A_CQ, _A_CKV, _A_KR, _A_FF = 0, 512, 1024, 1536, 1664, 1920, 2048, 2176
_NT = (((1,), (1,)), ((), ()))


def _row_tile(lp):
    best = 128
    for t in range(128, 513, 128):
        if lp % t == 0:
            best = t
    return best


def _whole(w, **kw):
    if isinstance(w, tuple):
        arr, layer = w
        return arr, pl.BlockSpec((None,) + arr.shape[1:], lambda *g: (layer,) + (0,) * (arr.ndim - 1), **kw)
    return w, pl.BlockSpec(w.shape, lambda *g: (0,) * w.ndim, **kw)


def _lane_iota(shape):
    return lax.broadcasted_iota(jnp.int32, shape, len(shape) - 1)


def _rms_rows(y, width):
    ss = jnp.sum(y * y, axis=-1, keepdims=True)
    return y * lax.rsqrt(ss * (1.0 / width) + EPS)


def _half_sums(y2):
    lane = _lane_iota(y2.shape)
    lo = jnp.sum(jnp.where(lane < HEAD_DIM, y2, 0.0), axis=-1, keepdims=True)
    hi = jnp.sum(jnp.where(lane >= HEAD_DIM, y2, 0.0), axis=-1, keepdims=True)
    return lo, hi


def _headnorm64_tile(yt):
    lo, hi = _half_sums(yt * yt)
    lane = _lane_iota(yt.shape)
    r = jnp.where(lane < HEAD_DIM, lax.rsqrt(lo * (1.0 / HEAD_DIM) + EPS), lax.rsqrt(hi * (1.0 / HEAD_DIM) + EPS))
    return yt * r


def _rotate_half(x):
    lane = _lane_iota(x.shape)
    half = ROPE_DIM // 2
    return jnp.where((lane % ROPE_DIM) < half, pltpu.roll(x, LANES - half, 1), pltpu.roll(x, half, 1))


def _quad_select(lane, vals):
    g = lane // ROPE_DIM
    out = jnp.where(g == 0, vals[0], vals[1])
    out = jnp.where(g == 2, vals[2], out)
    return jnp.where(g == 3, vals[3], out)


def _in_proj_kernel(h_ref, g1_ref, wa_ref, gfq_ref, gfk_ref, gsq_ref, gsk_ref, fb_ref,
                    gcq_ref, wqu_ref, gckv_ref, wkvu_ref, gmq_n_ref, gmq_r_ref, gmk_n_ref, gmk_r_ref,
                    wvtf_ref, wvtm_ref, wvts_ref,
                    cos_ref, sin_ref, tri_ref, eq_ref, ek_ref, qconst_ref, kconst_ref,
                    qf_ref, kf_ref, vtf_ref, qm_ref, km_ref, vtm_ref, sq_ref, sk_ref, vts_ref,
                    carry_ref, *, tm):
    j = pl.program_id(1)
    h = h_ref[...]
    xn = (_rms_rows(h, h.shape[-1]) * g1_ref[...]).astype(BF16)

    def proj(c0, width):
        return jnp.dot(xn, wa_ref[:, c0:c0 + width], preferred_element_type=F32)

    def store_vt(vt_ref, wt_ref, x):
        yt = lax.dot_general(wt_ref[...], x, _NT, preferred_element_type=F32)
        ones = jnp.ones((ONES_ROWS, tm), BF16)
        for hd in range(yt.shape[0] // HEAD_DIM):
            vt_ref[hd * VT_ROWS:hd * VT_ROWS + HEAD_DIM, :] = yt[hd * HEAD_DIM:(hd + 1) * HEAD_DIM, :].astype(BF16)
            vt_ref[hd * VT_ROWS + HEAD_DIM:(hd + 1) * VT_ROWS, :] = ones

    yq = proj(_A_FQ, 512)
    yk = proj(_A_FK, 512)
    for t in range(4):
        sl = slice(t * LANES, (t + 1) * LANES)
        qf_ref[:, 2 * t * LANES:(2 * t + 1) * LANES] = (_headnorm64_tile(yq[:, sl]) * gfq_ref[:, sl]).astype(BF16)
        kf_ref[:, 2 * t * LANES:(2 * t + 1) * LANES] = (_headnorm64_tile(yk[:, sl]) * gfk_ref[:, sl]).astype(BF16)
    store_vt(vtf_ref, wvtf_ref, xn)

    @pl.when(j == 0)
    def _():
        carry_ref[...] = jnp.zeros_like(carry_ref)

    z = proj(_A_FF, LANES) + fb_ref[...]
    ls = jnp.minimum(z, 0.0) - jnp.log1p(jnp.exp(-jnp.abs(z)))
    row = j * tm + lax.broadcasted_iota(jnp.int32, ls.shape, 0)
    ls = jnp.where((_lane_iota(ls.shape) < N_HEADS) & (row >= PAD), ls * LOG2E, 0.0)
    tri = tri_ref[...]
    c = carry_ref[0:1, :]
    rem = ls
    for _ in range(DEC_PARTS):
        part = rem.astype(BF16)
        c = c + jnp.dot(tri, part, preferred_element_type=F32)
        rem = rem - part.astype(F32)
    carry_ref[0:1, :] = c[tm - 1:tm, :]
    qd = qconst_ref[...]
    kd = kconst_ref[...]
    rem = c
    for i in range(DEC_PARTS):
        part = rem.astype(BF16)
        qd = qd + jnp.dot(part, eq_ref[i], preferred_element_type=F32)
        kd = kd + jnp.dot(part, ek_ref[i], preferred_element_type=F32)
        rem = rem - part.astype(F32)
    qd = qd.astype(BF16)
    kd = kd.astype(BF16)
    for t in range(4):
        qf_ref[:, (2 * t + 1) * LANES:(2 * t + 2) * LANES] = qd
        kf_ref[:, (2 * t + 1) * LANES:(2 * t + 2) * LANES] = kd

    ysq = proj(_A_SQ, 512)
    for t in range(4):
        sl = slice(t * LANES, (t + 1) * LANES)
        sq_ref[:, sl] = (_headnorm64_tile(ysq[:, sl]) * gsq_ref[:, sl]).astype(BF16)
    sk_ref[...] = (_headnorm64_tile(proj(_A_SK, LANES)) * gsk_ref[...]).astype(BF16)
    store_vt(vts_ref, wvts_ref, xn)

    cos = cos_ref[...]
    sin = sin_ref[...]

    def rope(x):
        return x * cos + _rotate_half(x) * sin

    cq = (_rms_rows(proj(_A_CQ, MLA_Q_RANK), MLA_Q_RANK) * gcq_ref[...]).astype(BF16)
    yq = jnp.dot(cq, wqu_ref[...], preferred_element_type=F32)
    lane = _lane_iota((tm, LANES))
    quad_id = lane // ROPE_DIM
    ss = []
    for t in range(4):
        lo, hi = _half_sums(jnp.square(yq[:, t * LANES:(t + 1) * LANES]))
        ss += [lo, hi]
    for u in range(2):
        y2 = jnp.square(yq[:, 512 + u * LANES:512 + (u + 1) * LANES])
        for g in range(4):
            ss[4 * u + g] = ss[4 * u + g] + jnp.sum(jnp.where(quad_id == g, y2, 0.0), axis=-1, keepdims=True)
    width = HEAD_DIM + ROPE_DIM
    rq = [lax.rsqrt(s * (1.0 / width) + EPS) for s in ss]
    q_rope = []
    for u in range(2):
        sl = slice(512 + u * LANES, 512 + (u + 1) * LANES)
        x = yq[:, sl] * _quad_select(lane, rq[4 * u:4 * u + 4]) * gmq_r_ref[...]
        q_rope.append(rope(x).astype(BF16))
    for t in range(4):
        sl = slice(t * LANES, (t + 1) * LANES)
        r = jnp.where(lane < HEAD_DIM, rq[2 * t], rq[2 * t + 1])
        qm_ref[:, 2 * t * LANES:(2 * t + 1) * LANES] = (yq[:, sl] * r * gmq_n_ref[:, sl]).astype(BF16)
        qm_ref[:, (2 * t + 1) * LANES:(2 * t + 2) * LANES] = q_rope[t // 2]

    ckv = (_rms_rows(proj(_A_CKV, MLA_KV_RANK), MLA_KV_RANK) * gckv_ref[...]).astype(BF16)
    ykv = jnp.dot(ckv, wkvu_ref[...], preferred_element_type=F32)
    kr4 = proj(_A_KR, LANES)
    ss_rope = jnp.sum(kr4 * kr4, axis=-1, keepdims=True) * 0.25
    rk = []
    for t in range(4):
        lo, hi = _half_sums(jnp.square(ykv[:, t * LANES:(t + 1) * LANES]))
        rk += [lax.rsqrt((lo + ss_rope) * (1.0 / width) + EPS), lax.rsqrt((hi + ss_rope) * (1.0 / width) + EPS)]
    kr_base = rope(kr4 * gmk_r_ref[...])
    k_rope = [(kr_base * _quad_select(lane, rk[4 * u:4 * u + 4])).astype(BF16) for u in range(2)]
    for t in range(4):
        sl = slice(t * LANES, (t + 1) * LANES)
        r = jnp.where(lane < HEAD_DIM, rk[2 * t], rk[2 * t + 1])
        km_ref[:, 2 * t * LANES:(2 * t + 1) * LANES] = (ykv[:, sl] * r * gmk_n_ref[:, sl]).astype(BF16)
        km_ref[:, (2 * t + 1) * LANES:(2 * t + 2) * LANES] = k_rope[t // 2]
    store_vt(vtm_ref, wvtm_ref, ckv)


def _in_proj(h, p, consts, tm):
    b, lp, d = h.shape
    nt = lp // tm
    row_blk = lambda w: pl.BlockSpec((None, tm, w), lambda bi, j: (bi, j, 0))
    full = lambda a: pl.BlockSpec(a.shape, lambda bi, j: (0,) * a.ndim)
    tab = pl.BlockSpec((tm, LANES), lambda bi, j: (j, 0))
    ins = [h, p['g1'], p['wa'], p['gfq'], p['gfk'], p['gsq'], p['gsk'], p['fb'],
           p['gcq'], p['wqu'], p['gckv'], p['wkvu'], p['gmq_n'], p['gmq_r'], p['gmk_n'], p['gmk_r'],
           p['wvtf'], p['wvtm'], p['wvts']]
    whole = [_whole(a) for a in ins[1:]]
    ins = [h] + [a for a, _ in whole]
    in_specs = [row_blk(d)] + [spec for _, spec in whole]
    ins += [consts['cos'], consts['sin'], consts['tri'], consts['eq'], consts['ek'], consts['qconst'], consts['kconst']]
    in_specs += [tab, tab] + [full(consts[k]) for k in ('tri', 'eq', 'ek', 'qconst', 'kconst')]
    outs = [('row', 1024), ('row', 1024), ('vt', N_HEADS), ('row', 1024), ('row', 1024), ('vt', N_HEADS),
            ('row', 512), ('row', 128), ('vt', SWA_KV_HEADS)]
    out_shape = [jax.ShapeDtypeStruct((b, lp, w) if kind == 'row' else (b, w * VT_ROWS, lp), BF16) for kind, w in outs]
    out_specs = [row_blk(w) if kind == 'row' else pl.BlockSpec((None, w * VT_ROWS, tm), lambda bi, j: (bi, 0, j))
                 for kind, w in outs]
    return pl.pallas_call(
        functools.partial(_in_proj_kernel, tm=tm),
        grid=(b, nt),
        in_specs=in_specs,
        out_specs=out_specs,
        out_shape=out_shape,
        scratch_shapes=[pltpu.VMEM((8, LANES), F32)],
        compiler_params=pltpu.CompilerParams(dimension_semantics=("arbitrary", "arbitrary"),
                                             vmem_limit_bytes=VMEM_LIMIT),
        name="in_proj",
    )(*ins)


def _causal_attn_kernel(mask_ref, q_ref, k_ref, vt_ref, o_ref, m_ref, acc_ref, qs_ref, s_ref, *sd_refs, tq, n_blocks):
    masks = mask_ref[...]

    def init(n):
        m_ref[:, :, 0:n] = jnp.full((2, 1, n), NEG, F32)
        acc_ref[:, :, 0:n] = jnp.zeros((2, VT_ROWS, n), F32)

    def update(unit, st, vt, visible):
        a, c0, nc = unit
        if visible is not None:
            st = jnp.where(visible, st, NEG)
        cols = slice(c0, c0 + nc)
        m_prev = m_ref[a, :, cols]
        m_new = jnp.maximum(m_prev, jnp.max(st, axis=0, keepdims=True))
        alpha = jnp.exp2(m_prev - m_new)
        p = jnp.exp2(st - m_new).astype(BF16)
        acc_ref[a, :, cols] = alpha * acc_ref[a, :, cols] + jnp.dot(vt, p, preferred_element_type=F32)
        m_ref[a, :, cols] = m_new

    def finish(r0, n):
        ot = jnp.concatenate([acc_ref[a, 0:HEAD_DIM, 0:n] * (1.0 / acc_ref[a, HEAD_DIM:HEAD_DIM + 1, 0:n])
                              for a in range(2)], axis=0)
        o_ref[pl.ds(r0, n), :] = ot.T.astype(o_ref.dtype)

    def set_queries(qslot, r0, n):
        q = q_ref[pl.ds(r0, n), :]
        for a in range(2):
            qs_ref[qslot, a, 0:n, :] = q * masks[a:a + 1, :]

    def scores(qslot, unit, keys):
        a, c0, nc = unit
        return lax.dot_general(keys, qs_ref[qslot, a, c0:c0 + nc, :], _NT, preferred_element_type=F32)

    def values(a, k0, nk):
        return vt_ref[a * VT_ROWS:(a + 1) * VT_ROWS, pl.ds(k0, nk)]

    def visible(shape, c0, causal_lead):
        row = lax.broadcasted_iota(jnp.int32, shape, 0)
        col = lax.broadcasted_iota(jnp.int32, shape, 1)
        lead = row < FRONT
        lowest = jnp.where(lead, PAD, FRONT)
        highest = jnp.where(lead, col if causal_lead else FRONT, col + (c0 + FRONT))
        return (row >= lowest) & (row <= highest)

    init(FRONT)
    set_queries(0, 0, FRONT)
    lead_units = [(0, 0, FRONT), (1, 0, FRONT)]
    lead_keys = k_ref[0:FRONT, :]
    lead_sts = [scores(0, u, lead_keys) for u in lead_units]
    for u, st in zip(lead_units, lead_sts):
        update(u, st, values(u[0], 0, FRONT), visible(st.shape, 0, True))
    finish(0, FRONT)

    n_sub = tq // QUERY_SUB
    units = [(a, s * QUERY_SUB, QUERY_SUB) for a in range(2) for s in range(n_sub)]
    diag_keys = [FRONT + unit[1] + QUERY_SUB for unit in units]

    def block_start(j):
        return pl.multiple_of(FRONT + j * tq, LANES)

    def park_block(slot, qslot, j):
        keys = k_ref[pl.ds(block_start(j), tq), :]
        for u, unit in enumerate(units):
            s_ref[slot, u] = scores(qslot, unit, keys)

    def consume_block(slot, j):
        for u, unit in enumerate(units):
            update(unit, s_ref[slot, u], values(unit[0], block_start(j), tq), None)

    def park_diag(qslot, i):
        kk = jnp.concatenate([k_ref[0:FRONT, :], k_ref[pl.ds(block_start(i), tq), :]], axis=0)
        for u, (unit, nk) in enumerate(zip(units, diag_keys)):
            sd_refs[u][...] = scores(qslot, unit, kk[0:nk, :])

    def consume_diag(i):
        vts = [jnp.concatenate([values(a, 0, FRONT), values(a, block_start(i), tq)], axis=1) for a in range(2)]
        for u, (unit, nk) in enumerate(zip(units, diag_keys)):
            update(unit, sd_refs[u][...], vts[unit[0]][:, 0:nk], visible((nk, unit[2]), unit[1], False))

    def visible_pairs(qslot, i, n_pairs):
        def pair(t, c):
            park_block(1, qslot, 2 * t + 1)
            consume_block(0, 2 * t)
            park_block(0, qslot, jnp.minimum(2 * t + 2, i - 1))
            consume_block(1, 2 * t + 1)
            return c
        lax.fori_loop(0, n_pairs, pair, 0)

    set_queries(0, block_start(0), tq)

    def two_blocks(g, carry):
        even, odd = 2 * g, 2 * g + 1
        init(tq)
        visible_pairs(0, even, g)
        set_queries(1, block_start(odd), tq)
        park_diag(0, even)
        park_block(0, 1, 0)
        consume_diag(even)
        finish(block_start(even), tq)
        init(tq)
        visible_pairs(1, odd, g)
        park_diag(1, odd)
        consume_block(0, 2 * g)
        nxt = jnp.minimum(even + 2, n_blocks - 1)
        set_queries(0, block_start(nxt), tq)
        park_block(0, 0, 0)
        consume_diag(odd)
        finish(block_start(odd), tq)
        return carry

    lax.fori_loop(0, n_blocks // 2, two_blocks, 0)


def _causal_attn(q, k, vt, masks, tq):
    b, _, lp = vt.shape
    n_pairs = N_HEADS // 2
    n_blocks = (lp - FRONT) // tq
    assert tq % QUERY_SUB == 0 and n_blocks % 2 == 0
    n_sub = tq // QUERY_SUB
    n_units = 2 * n_sub

    def qk_spec():
        return pl.BlockSpec((None, lp, 2 * LANES), lambda bi, p: (bi, 0, p))

    return pl.pallas_call(
        functools.partial(_causal_attn_kernel, tq=tq, n_blocks=n_blocks),
        grid=(b, n_pairs),
        in_specs=[pl.BlockSpec((None, 2, 2 * LANES), lambda bi, p: (p, 0, 0)),
                  qk_spec(), qk_spec(),
                  pl.BlockSpec((None, 2 * VT_ROWS, lp), lambda bi, p: (bi, p, 0))],
        out_specs=pl.BlockSpec((None, lp, LANES), lambda bi, p: (bi, 0, p)),
        out_shape=jax.ShapeDtypeStruct((b, lp, BRANCH_WIDTH), BF16),
        scratch_shapes=[pltpu.VMEM((2, 1, tq), F32), pltpu.VMEM((2, VT_ROWS, tq), F32),
                        pltpu.VMEM((2, 2, tq, 2 * LANES), BF16),
                        pltpu.VMEM((2, n_units, tq, QUERY_SUB), F32),
                        *[pltpu.VMEM((FRONT + (u % n_sub + 1) * QUERY_SUB, QUERY_SUB), F32) for u in range(n_units)]],
        compiler_params=pltpu.CompilerParams(dimension_semantics=("arbitrary", "arbitrary"),
                                             vmem_limit_bytes=VMEM_LIMIT),
        name="causal_attn",
    )(masks, q, k, vt)


def _swa_kernel(sinks_ref, mask_ref, q_ref, qaux_ref, k_ref, kaux_ref, vt_ref, o_ref, bias_ref, s_ref, *, qb, n_blocks):
    masks = mask_ref[...]
    n_band = WINDOW + qb

    def key_bias(shape, n_band_rows, band_shift):
        row = lax.broadcasted_iota(jnp.int32, shape, 0)
        col = lax.broadcasted_iota(jnp.int32, shape, 1)
        band = row < n_band_rows
        lowest = jnp.where(band, col + (band_shift + 1), 0)
        highest = jnp.where(band, col + (band_shift + WINDOW), n_band_rows + N_META)
        return jnp.where((row >= lowest) & (row <= highest), 0.0, NEG)

    def keys(key_slices):
        return jnp.concatenate([jnp.concatenate([k_ref[pl.ds(s, n), :], kaux_ref[pl.ds(s, n), :]], axis=1)
                                for s, n in key_slices], axis=0)

    def values(key_slices):
        return [jnp.concatenate([vt_ref[g * VT_ROWS:(g + 1) * VT_ROWS, pl.ds(s, n)] for s, n in key_slices], axis=1)
                for g in range(SWA_KV_HEADS)]

    def tile_scores(r0, nq, t, kk):
        cols = slice(t * LANES, (t + 1) * LANES)
        qt = jnp.concatenate([q_ref[pl.ds(r0, nq), cols], qaux_ref[pl.ds(r0, nq), cols]], axis=1)
        return [lax.dot_general(kk, qt * masks[a:a + 1, :], _NT, preferred_element_type=F32) for a in range(2)]

    def tile_finish(r0, nq, t, sts, vts, bias):
        outs = []
        for a in range(2):
            sink = sinks_ref[t + 4 * a]
            st = sts[a] + bias
            m = jnp.maximum(jnp.max(st, axis=0, keepdims=True), sink)
            p = jnp.exp(st - m).astype(BF16)
            acc = jnp.dot(vts[a], p, preferred_element_type=F32)
            den = acc[HEAD_DIM:HEAD_DIM + 1, :] + jnp.exp(sink - m)
            outs.append(acc[0:HEAD_DIM, :] * (1.0 / den))
        o_ref[pl.ds(r0, nq), t * LANES:(t + 1) * LANES] = jnp.concatenate(outs, axis=0).T.astype(o_ref.dtype)

    def attend(r0, nq, key_slices, bias):
        kk, vts = keys(key_slices), values(key_slices)
        for t in range(4):
            tile_finish(r0, nq, t, tile_scores(r0, nq, t, kk), vts, bias)

    row = lax.broadcasted_iota(jnp.int32, (FRONT, FRONT), 0)
    col = lax.broadcasted_iota(jnp.int32, (FRONT, FRONT), 1)
    attend(0, FRONT, [(0, FRONT)], jnp.where((row >= PAD) & (row <= col), 0.0, NEG))
    meta = (PAD, N_META)
    attend(FRONT, qb, [(FRONT, qb), meta], key_bias((qb + N_META, qb), qb, -WINDOW))
    bias_ref[...] = key_bias(bias_ref.shape, n_band, 0)

    def block_keys(i):
        return [(pl.multiple_of(FRONT + i * qb - WINDOW, LANES), n_band), meta]

    def park_scores(slot, i, t):
        sts = tile_scores(pl.multiple_of(FRONT + i * qb, LANES), qb, t, keys(block_keys(i)))
        for a in range(2):
            s_ref[slot, a] = sts[a]

    park_scores(0, 1, 0)

    def q_block(i, carry):
        r0 = pl.multiple_of(FRONT + i * qb, LANES)
        vts = values(block_keys(i))
        for t in range(4):
            if t < 3:
                park_scores((t + 1) % 2, i, t + 1)
            else:
                park_scores(0, jnp.minimum(i + 1, n_blocks - 1), 0)
            tile_finish(r0, qb, t, [s_ref[t % 2, a] for a in range(2)], vts, bias_ref[...])
        return carry

    lax.fori_loop(1, n_blocks, q_block, 0)


def _swa_attn(q, k, vt, sinks, consts):
    b, lp, _ = q.shape
    qb = SWA_QUERY_BLOCK
    assert (lp - FRONT) % qb == 0
    full = lambda a: pl.BlockSpec(a.shape, lambda bi: (0,) * a.ndim)
    per_batch = lambda r, c: pl.BlockSpec((None, r, c), lambda bi: (bi, 0, 0))
    return pl.pallas_call(
        functools.partial(_swa_kernel, qb=qb, n_blocks=(lp - FRONT) // qb),
        grid=(b,),
        in_specs=[pl.BlockSpec(memory_space=pltpu.SMEM), full(consts['swa_masks']),
                  per_batch(lp, BRANCH_WIDTH), full(consts['swa_qaux']),
                  per_batch(lp, LANES), full(consts['swa_kaux']),
                  per_batch(SWA_KV_HEADS * VT_ROWS, lp)],
        out_specs=per_batch(lp, BRANCH_WIDTH),
        out_shape=jax.ShapeDtypeStruct((b, lp, BRANCH_WIDTH), BF16),
        scratch_shapes=[pltpu.VMEM((WINDOW + qb + N_META, qb), F32),
                        pltpu.VMEM((2, 2, WINDOW + qb + N_META, qb), F32)],
        compiler_params=pltpu.CompilerParams(dimension_semantics=("arbitrary",), vmem_limit_bytes=VMEM_LIMIT),
        name="swa_attn",
    )(sinks, consts['swa_masks'], q, consts['swa_qaux'], k, consts['swa_kaux'], vt)


def _merge_kernel(h_ref, oa_ref, ob_ref, oc_ref, g1_ref, wg_ref, wb_ref, wo_ref, out_ref):
    h = h_ref[...]
    d = h.shape[-1]
    xn = (_rms_rows(h, d) * g1_ref[...]).astype(BF16)
    merged = None
    for n, o_ref in enumerate((oa_ref, ob_ref, oc_ref)):
        gate = jax.nn.sigmoid(jnp.dot(xn, wg_ref[:, n * d:(n + 1) * d], preferred_element_type=F32))
        y = jnp.dot(o_ref[...], wb_ref[n], preferred_element_type=F32)
        merged = gate * y if merged is None else merged + gate * y
    out_ref[...] = h + jnp.dot(merged.astype(BF16), wo_ref[...], preferred_element_type=F32)


def _merge(h, oa, ob, oc, p, tm):
    b, lp, d = h.shape
    row_blk = lambda w: pl.BlockSpec((None, tm, w), lambda bi, j: (bi, j, 0))
    whole = [_whole(p[k], pipeline_mode=pl.Buffered(1)) for k in ('g1', 'wg', 'wb', 'wo')]
    ws = [a for a, _ in whole]
    return pl.pallas_call(
        _merge_kernel,
        grid=(b, lp // tm),
        in_specs=[row_blk(d), row_blk(BRANCH_WIDTH), row_blk(BRANCH_WIDTH), row_blk(BRANCH_WIDTH)]
        + [spec for _, spec in whole],
        out_specs=row_blk(d),
        out_shape=jax.ShapeDtypeStruct(h.shape, F32),
        compiler_params=pltpu.CompilerParams(dimension_semantics=("arbitrary", "arbitrary"),
                                             vmem_limit_bytes=VMEM_LIMIT),
        name="merge",
    )(h, oa, ob, oc, *ws)


_FF_CHUNK = 256
_HALO = 8


def _ffn_kernel(h_ref, prev_ref, g2_ref, wup_ref, cw_ref, cb_ref, wdn_ref, out_ref, carry_ref, ubuf_ref, act_ref, *,
                tm, d_ff, real_rows_only):
    j = pl.program_id(1)
    h = h_ref[...]
    xn = _rms_rows(h, h.shape[-1]) * g2_ref[...]
    if not real_rows_only:
        row = j * tm + lax.broadcasted_iota(jnp.int32, (tm, 1), 0)
        xn = jnp.where(row >= PAD, xn, 0.0)
    xn = xn.astype(BF16)

    @pl.when(j == 0)
    def _():
        if real_rows_only:
            prev = prev_ref[...]
            xp = (_rms_rows(prev, prev.shape[-1]) * g2_ref[...]).astype(BF16)
            u_prev = jnp.dot(xp, wup_ref[...], preferred_element_type=F32)
            carry_ref[...] = u_prev[N_META - _HALO:N_META, :]
        else:
            carry_ref[...] = jnp.zeros_like(carry_ref)

    for c in range(d_ff // _FF_CHUNK):
        acts = []
        for half in range(2):
            c0 = half * d_ff + c * _FF_CHUNK
            cols = slice(c0, c0 + _FF_CHUNK)
            u = jnp.dot(xn, wup_ref[:, cols], preferred_element_type=F32)
            ubuf_ref[half, 0:_HALO, :] = carry_ref[:, cols]
            ubuf_ref[half, _HALO:_HALO + tm, :] = u
            carry_ref[:, cols] = u[tm - _HALO:tm, :]
            u1 = ubuf_ref[half, _HALO - 1:_HALO - 1 + tm, :]
            u2 = ubuf_ref[half, _HALO - 2:_HALO - 2 + tm, :]
            acts.append(cb_ref[:, cols] + cw_ref[2:3, cols] * u + cw_ref[1:2, cols] * u1 + cw_ref[0:1, cols] * u2)
        gate, val = acts
        act_ref[:, c * _FF_CHUNK:(c + 1) * _FF_CHUNK] = (gate * jax.nn.sigmoid(gate) * val).astype(BF16)
    out_ref[...] = h + jnp.dot(act_ref[...], wdn_ref[...], preferred_element_type=F32)


def _ffn(h, p, tm, real_rows_only=False):
    b, lp, d = h.shape
    row_blk = pl.BlockSpec((None, tm, d), lambda bi, j: (bi, j, 0))
    n_rows = lp - FRONT if real_rows_only else lp
    if real_rows_only:
        h_spec = pl.BlockSpec((None, pl.Element(tm), pl.Element(d)),
                              lambda bi, j: (bi, pl.multiple_of(FRONT + j * tm, LANES), 0))
        prev_spec = pl.BlockSpec((None, pl.Element(N_META), pl.Element(d)), lambda bi, j: (bi, PAD, 0))
    else:
        h_spec = row_blk
        prev_spec = pl.BlockSpec((None, N_META, d), lambda bi, j: (bi, 0, 0))
    whole = [_whole(p[k], pipeline_mode=pl.Buffered(1)) for k in ('g2', 'wup', 'cw', 'cb', 'wdn')]
    ws = [a for a, _ in whole]
    d_ff = ws[-1].shape[-2]
    return pl.pallas_call(
        functools.partial(_ffn_kernel, tm=tm, d_ff=d_ff, real_rows_only=real_rows_only),
        grid=(b, n_rows // tm),
        in_specs=[h_spec, prev_spec] + [spec for _, spec in whole],
        out_specs=row_blk,
        out_shape=jax.ShapeDtypeStruct((b, n_rows, d), F32),
        scratch_shapes=[pltpu.VMEM((_HALO, 2 * d_ff), F32),
                        pltpu.VMEM((2, _HALO + tm, _FF_CHUNK), F32),
                        pltpu.VMEM((tm, d_ff), BF16)],
        compiler_params=pltpu.CompilerParams(dimension_semantics=("arbitrary", "arbitrary"),
                                             vmem_limit_bytes=VMEM_LIMIT),
        name="conv_ffn",
    )(h, h, *ws)


def _constants(lp, tm):
    half = ROPE_DIM // 2
    freqs = ROPE_THETA ** (-np.arange(half, dtype=np.float64) / half)
    lane = np.arange(LANES)
    ang = (np.arange(lp) - PAD).astype(np.float64)[:, None] * freqs[lane % half][None, :]
    sign = np.where((lane % ROPE_DIM) < half, -1.0, 1.0)
    row = np.arange(lp)
    pos_hi, pos_lo = row // LANES, row % LANES
    swa_kaux = np.zeros((lp, LANES), np.float32)
    swa_qaux = np.zeros((lp, BRANCH_WIDTH), np.float32)
    swa_masks = np.zeros((2, 2 * LANES), np.float32)
    for a in range(2):
        swa_masks[a, a * HEAD_DIM:(a + 1) * HEAD_DIM] = 1.0
        swa_masks[a, LANES + 4 * a:LANES + 4 * (a + 1)] = 1.0
        swa_kaux[:, 4 * a:4 * (a + 1)] = np.stack([np.ones(lp), np.ones(lp), pos_hi, pos_lo], axis=1)
        for t in range(4):
            slope = 2.0 ** (-8.0 * (t + 4 * a + 1) / N_HEADS)
            swa_qaux[:, t * LANES + 4 * a:t * LANES + 4 * (a + 1)] = np.stack(
                [-LANES * slope * pos_hi, -slope * pos_lo, np.full(lp, LANES * slope), np.full(lp, slope)], axis=1)
    tri = np.tril(np.ones((tm, tm), np.float32))
    eq = np.zeros((DEC_PARTS, LANES, LANES), np.float32)
    ek = np.zeros((DEC_PARTS, LANES, LANES), np.float32)
    qconst = np.zeros((1, LANES), np.float32)
    kconst = np.zeros((1, LANES), np.float32)
    for hd in range(N_HEADS):
        base = 2 * DEC_PARTS * hd
        for i in range(DEC_PARTS):
            eq[i, hd, base + i] = 1.0
            ek[i, hd, base + DEC_PARTS + i] = -1.0
            qconst[0, base + DEC_PARTS + i] = 1.0
            kconst[0, base + i] = 1.0
    fox_masks = np.zeros((N_HEADS // 2, 2, 2 * LANES), np.float32)
    mla_masks = np.zeros((N_HEADS // 2, 2, 2 * LANES), np.float32)
    for p in range(N_HEADS // 2):
        for a in range(2):
            hd = 2 * p + a
            fox_masks[p, a, a * HEAD_DIM:(a + 1) * HEAD_DIM] = 1.0
            fox_masks[p, a, LANES + 2 * DEC_PARTS * hd:LANES + 2 * DEC_PARTS * (hd + 1)] = 1.0
            mla_masks[p, a, a * HEAD_DIM:(a + 1) * HEAD_DIM] = 1.0
            g = hd % 4
            mla_masks[p, a, LANES + g * ROPE_DIM:LANES + (g + 1) * ROPE_DIM] = 1.0
    return {
        'cos': jnp.asarray(np.cos(ang), F32), 'sin': jnp.asarray(np.sin(ang) * sign[None, :], F32),
        'tri': jnp.asarray(tri, BF16), 'eq': jnp.asarray(eq, BF16), 'ek': jnp.asarray(ek, BF16),
        'qconst': jnp.asarray(qconst), 'kconst': jnp.asarray(kconst),
        'fox_masks': jnp.asarray(fox_masks, BF16), 'mla_masks': jnp.asarray(mla_masks, BF16),
        'swa_masks': jnp.asarray(swa_masks, BF16), 'swa_qaux': jnp.asarray(swa_qaux, BF16),
        'swa_kaux': jnp.asarray(swa_kaux, BF16),
    }


def _swa_head_order(a, axis):
    shape = a.shape
    a = a.reshape(shape[:axis] + (SWA_KV_HEADS, N_HEADS // SWA_KV_HEADS, HEAD_DIM) + shape[axis + 1:])
    return jnp.swapaxes(a, axis, axis + 1).reshape(shape)


_O_FQ, _O_FK, _O_FV, _O_FF = 0, 512, 1024, 1536
_O_CQ = _O_FF + N_HEADS
_O_CKV = _O_CQ + MLA_Q_RANK
_O_KR = _O_CKV + MLA_KV_RANK
_O_SQ = _O_KR + ROPE_DIM
_O_SK = _O_SQ + 512
_O_SV = _O_SK + SWA_KV_HEADS * HEAD_DIM
_O_GATES = _O_SV + SWA_KV_HEADS * HEAD_DIM


def _stacked_weights(w_in, w_branch, w_o, ffn_w_up, ffn_w_down):
    n_layers, d, _ = w_in.shape
    col = lambda c0, n: w_in[:, :, c0:c0 + n]
    wa = jnp.concatenate([
        col(_O_FQ, 512), col(_O_FK, 512), _swa_head_order(col(_O_SQ, 512), 2), col(_O_SK, 128),
        col(_O_CQ, MLA_Q_RANK), col(_O_CKV, MLA_KV_RANK),
        jnp.tile(col(_O_KR, ROPE_DIM), (1, 1, LANES // ROPE_DIM)),
        col(_O_FF, N_HEADS), jnp.zeros((n_layers, d, LANES - N_HEADS), w_in.dtype)], axis=2).astype(BF16)
    wb = jnp.concatenate([w_branch[:, :2], _swa_head_order(w_branch[:, 2:3], 2)], axis=1).astype(BF16)
    return {'wa': wa, 'wg': w_in[:, :, _O_GATES:].astype(BF16), 'wb': wb, 'wo': w_o.astype(BF16),
            'wup': ffn_w_up.astype(BF16), 'wdn': ffn_w_down.astype(BF16)}


def _layer_params(l, stacked, norm1_g, w_in, fox_forget_b, fox_q_g, fox_k_g, mla_q_a_g, mla_w_q_up, mla_kv_a_g,
                  mla_w_kv_up, mla_q_g, mla_k_g, swa_q_g, swa_k_g, swa_sinks, w_branch, w_o,
                  norm2_g, ffn_w_up, ffn_conv_w, ffn_conv_b, ffn_w_down):
    w = w_in[l]
    tile_row = lambda g, n: jnp.tile(g, n)[None, :].astype(F32)
    width = HEAD_DIM + ROPE_DIM
    qcols = np.concatenate([np.concatenate([np.arange(hd * width, hd * width + HEAD_DIM) for hd in range(N_HEADS)]),
                            np.concatenate([np.arange(hd * width + HEAD_DIM, (hd + 1) * width) for hd in range(N_HEADS)])])
    kcols = np.concatenate([np.arange(hd * 2 * HEAD_DIM, hd * 2 * HEAD_DIM + HEAD_DIM) for hd in range(N_HEADS)])
    vcols = kcols + HEAD_DIM
    fb = jnp.concatenate([fox_forget_b[l], jnp.zeros((LANES - N_HEADS,), F32)])[None, :]
    return {
        'g1': norm1_g[l][None, :], 'wa': (stacked['wa'], l),
        'gfq': tile_row(fox_q_g[l], N_HEADS) * (LOG2E * HEAD_DIM ** -0.5), 'gfk': tile_row(fox_k_g[l], N_HEADS),
        'gsq': tile_row(swa_q_g[l], N_HEADS) * (HEAD_DIM ** -0.5), 'gsk': tile_row(swa_k_g[l], SWA_KV_HEADS),
        'fb': fb,
        'gcq': mla_q_a_g[l][None, :], 'wqu': mla_w_q_up[l][:, qcols].astype(BF16),
        'gckv': mla_kv_a_g[l][None, :], 'wkvu': mla_w_kv_up[l][:, kcols].astype(BF16),
        'wvtf': w[:, _O_FV:_O_FV + 512].T.astype(BF16), 'wvtm': mla_w_kv_up[l][:, vcols].T.astype(BF16),
        'wvts': w[:, _O_SV:_O_SV + SWA_KV_HEADS * HEAD_DIM].T.astype(BF16),
        'gmq_n': tile_row(mla_q_g[l][:HEAD_DIM], N_HEADS) * (LOG2E * width ** -0.5),
        'gmq_r': tile_row(mla_q_g[l][HEAD_DIM:], LANES // ROPE_DIM) * (LOG2E * width ** -0.5),
        'gmk_n': tile_row(mla_k_g[l][:HEAD_DIM], N_HEADS),
        'gmk_r': tile_row(mla_k_g[l][HEAD_DIM:], LANES // ROPE_DIM),
        'sinks': swa_sinks[l].astype(F32),
        'wg': (stacked['wg'], l), 'wb': (stacked['wb'], l), 'wo': (stacked['wo'], l),
        'g2': norm2_g[l][None, :], 'wup': (stacked['wup'], l), 'cw': ffn_conv_w[l], 'cb': ffn_conv_b[l][None, :],
        'wdn': (stacked['wdn'], l),
    }


def kernel(x, meta_tokens, norm1_g, w_in, fox_forget_b, fox_q_g, fox_k_g, mla_q_a_g, mla_w_q_up, mla_kv_a_g, mla_w_kv_up, mla_q_g, mla_k_g, swa_q_g, swa_k_g, swa_sinks, w_branch, w_o, norm2_g, ffn_w_up, ffn_conv_w, ffn_conv_b, ffn_w_down):
    b, seq, d = x.shape
    assert seq % LANES == 0 and meta_tokens.shape == (N_META, d)
    lp = FRONT + seq
    tm = _row_tile(lp)
    tq = CAUSAL_BLOCK if seq % (2 * CAUSAL_BLOCK) == 0 else _row_tile(seq)
    consts = _constants(lp, tm)
    front = jnp.concatenate([jnp.zeros((PAD, d), x.dtype), meta_tokens.astype(x.dtype)], axis=0)
    h = jnp.concatenate([jnp.broadcast_to(front[None], (b, FRONT, d)), x], axis=1)
    weights = (norm1_g, w_in, fox_forget_b, fox_q_g, fox_k_g, mla_q_a_g, mla_w_q_up, mla_kv_a_g, mla_w_kv_up,
               mla_q_g, mla_k_g, swa_q_g, swa_k_g, swa_sinks, w_branch, w_o, norm2_g, ffn_w_up, ffn_conv_w,
               ffn_conv_b, ffn_w_down)
    stacked = _stacked_weights(w_in, w_branch, w_o, ffn_w_up, ffn_w_down)
    for l in range(w_in.shape[0]):
        p = _layer_params(l, stacked, *weights)
        qf, kf, vtf, qm, km, vtm, sq, sk, vts = _in_proj(h, p, consts, tm)
        out_a = _causal_attn(qf, kf, vtf, consts['fox_masks'], tq)
        out_b = _causal_attn(qm, km, vtm, consts['mla_masks'], tq)
        out_c = _swa_attn(sq, sk, vts, p['sinks'], consts)
        h = _merge(h, out_a, out_b, out_c, p, tm)
        last = l == w_in.shape[0] - 1
        h = _ffn(h, p, _row_tile(seq) if last else tm, real_rows_only=last)
    return h
```

```python
import functools
import math

import numpy as np
import jax
import jax.numpy as jnp
from jax import lax
from jax.experimental import pallas as pl
from jax.experimental.pallas import tpu as pltpu

F32 = jnp.float32
BF16 = jnp.bfloat16

N_META = 16
EPS = 1e-6
N_HEADS = 8
HEAD_DIM = 64
ROPE_DIM = 32
MLA_Q_RANK = 256
MLA_KV_RANK = 128
SWA_KV_HEADS = 2
WINDOW = 128
ROPE_THETA = 10000.0
BRANCH_WIDTH = N_HEADS * HEAD_DIM
N_BRANCH = 3
CONV_WIDTH = 3

LANES = 128
FRONT = 128
PAD = FRONT - N_META
NEG = -1e30
DEC_PARTS = 3
LOG2E = math.log2(math.e)
ONES_ROWS = 16
VT_ROWS = HEAD_DIM + ONES_ROWS
QUERY_SUB = 256
CAUSAL_BLOCK = 1024
IN_SUB = 128
SWA_QUERY_BLOCK = 256
VMEM_LIMIT = 56 * 1024 * 1024

_A_FQ, _A_FK, _A_SQ, _A_SK, _A_CQ, _A_CKV, _A_KR, _A_FF = 0, 512, 1024, 1536, 1664, 1920, 2048, 2176
_NT = (((1,), (1,)), ((), ()))


def _row_tile(lp):
    best = 128
    for t in range(128, 513, 128):
        if lp % t == 0:
            best = t
    return best


def _whole(w, **kw):
    if isinstance(w, tuple):
        arr, layer = w
        return arr, pl.BlockSpec((None,) + arr.shape[1:], lambda *g: (layer,) + (0,) * (arr.ndim - 1), **kw)
    return w, pl.BlockSpec(w.shape, lambda *g: (0,) * w.ndim, **kw)


def _lane_iota(shape):
    return lax.broadcasted_iota(jnp.int32, shape, len(shape) - 1)


def _rms_rows(y, width):
    ss = jnp.sum(y * y, axis=-1, keepdims=True)
    return y * lax.rsqrt(ss * (1.0 / width) + EPS)


def _half_sums(y2):
    lane = _lane_iota(y2.shape)
    lo = jnp.sum(jnp.where(lane < HEAD_DIM, y2, 0.0), axis=-1, keepdims=True)
    hi = jnp.sum(jnp.where(lane >= HEAD_DIM, y2, 0.0), axis=-1, keepdims=True)
    return lo, hi


def _headnorm64_tile(yt):
    lo, hi = _half_sums(yt * yt)
    lane = _lane_iota(yt.shape)
    r = jnp.where(lane < HEAD_DIM, lax.rsqrt(lo * (1.0 / HEAD_DIM) + EPS), lax.rsqrt(hi * (1.0 / HEAD_DIM) + EPS))
    return yt * r


def _rotate_half(x):
    lane = _lane_iota(x.shape)
    half = ROPE_DIM // 2
    return jnp.where((lane % ROPE_DIM) < half, pltpu.roll(x, LANES - half, 1), pltpu.roll(x, half, 1))


def _quad_select(lane, vals):
    g = lane // ROPE_DIM
    out = jnp.where(g == 0, vals[0], vals[1])
    out = jnp.where(g == 2, vals[2], out)
    return jnp.where(g == 3, vals[3], out)


def _in_proj_kernel(h_ref, g1_ref, wa_ref, gfq_ref, gfk_ref, gsq_ref, gsk_ref, fb_ref,
                    gcq_ref, wqu_ref, gckv_ref, wkvu_ref, gmq_n_ref, gmq_r_ref, gmk_n_ref, gmk_r_ref,
                    wvtf_ref, wvtm_ref, wvts_ref,
                    cos_ref, sin_ref, tri_ref, eq_ref, ek_ref, qconst_ref, kconst_ref,
                    qf_ref, kf_ref, vtf_ref, qm_ref, km_ref, vtm_ref, sq_ref, sk_ref, vts_ref,
                    carry_ref, xn_ref, y_ref, *, tm):
    j = pl.program_id(1)
    h = h_ref[...]
    xn_ref[...] = (_rms_rows(h, h.shape[-1]) * g1_ref[...]).astype(BF16)

    def store_vt(vt_ref, wt_ref, x, cols):
        yt = lax.dot_general(wt_ref[...], x, _NT, preferred_element_type=F32)
        ones = jnp.ones((ONES_ROWS, yt.shape[1]), BF16)
        for hd in range(yt.shape[0] // HEAD_DIM):
            vt_ref[hd * VT_ROWS:hd * VT_ROWS + HEAD_DIM, cols] = yt[hd * HEAD_DIM:(hd + 1) * HEAD_DIM, :].astype(BF16)
            vt_ref[hd * VT_ROWS + HEAD_DIM:(hd + 1) * VT_ROWS, cols] = ones

    store_vt(vtf_ref, wvtf_ref, xn_ref[...], slice(0, tm))
    store_vt(vts_ref, wvts_ref, xn_ref[...], slice(0, tm))

    @pl.when(j == 0)
    def _():
        carry_ref[...] = jnp.zeros_like(carry_ref)

    n = IN_SUB
    lane = _lane_iota((n, LANES))
    quad_id = lane // ROPE_DIM
    width = HEAD_DIM + ROPE_DIM

    def project(s):
        y_ref[s % 2] = jnp.dot(xn_ref[s * n:(s + 1) * n, :], wa_ref[...], preferred_element_type=F32)

    def finish(s):
        rows = slice(s * n, (s + 1) * n)

        def proj(c0, w):
            return y_ref[s % 2, :, c0:c0 + w]

        yq = proj(_A_FQ, 512)
        yk = proj(_A_FK, 512)
        for t in range(4):
            sl = slice(t * LANES, (t + 1) * LANES)
            qf_ref[rows, 2 * t * LANES:(2 * t + 1) * LANES] = (_headnorm64_tile(yq[:, sl]) * gfq_ref[:, sl]).astype(BF16)
            kf_ref[rows, 2 * t * LANES:(2 * t + 1) * LANES] = (_headnorm64_tile(yk[:, sl]) * gfk_ref[:, sl]).astype(BF16)

        z = proj(_A_FF, LANES) + fb_ref[...]
        ls = jnp.minimum(z, 0.0) - jnp.log1p(jnp.exp(-jnp.abs(z)))
        row = j * tm + s * n + lax.broadcasted_iota(jnp.int32, ls.shape, 0)
        ls = jnp.where((lane < N_HEADS) & (row >= PAD), ls * LOG2E, 0.0)
        tri = tri_ref[...]
        c = carry_ref[0:1, :]
        rem = ls
        for _ in range(DEC_PARTS):
            part = rem.astype(BF16)
            c = c + jnp.dot(tri, part, preferred_element_type=F32)
            rem = rem - part.astype(F32)
        carry_ref[0:1, :] = c[n - 1:n, :]
        qd = qconst_ref[...]
        kd = kconst_ref[...]
        rem = c
        for i in range(DEC_PARTS):
            part = rem.astype(BF16)
            qd = qd + jnp.dot(part, eq_ref[i], preferred_element_type=F32)
            kd = kd + jnp.dot(part, ek_ref[i], preferred_element_type=F32)
            rem = rem - part.astype(F32)
        qd = qd.astype(BF16)
        kd = kd.astype(BF16)
        for t in range(4):
            qf_ref[rows, (2 * t + 1) * LANES:(2 * t + 2) * LANES] = qd
            kf_ref[rows, (2 * t + 1) * LANES:(2 * t + 2) * LANES] = kd

        ysq = proj(_A_SQ, 512)
        for t in range(4):
            sl = slice(t * LANES, (t + 1) * LANES)
            sq_ref[rows, sl] = (_headnorm64_tile(ysq[:, sl]) * gsq_ref[:, sl]).astype(BF16)
        sk_ref[rows, :] = (_headnorm64_tile(proj(_A_SK, LANES)) * gsk_ref[...]).astype(BF16)

        cos = cos_ref[rows, :]
        sin = sin_ref[rows, :]

        def rope(x):
            return x * cos + _rotate_half(x) * sin

        cq = (_rms_rows(proj(_A_CQ, MLA_Q_RANK), MLA_Q_RANK) * gcq_ref[...]).astype(BF16)
        yq = jnp.dot(cq, wqu_ref[...], preferred_element_type=F32)
        ss = []
        for t in range(4):
            lo, hi = _half_sums(jnp.square(yq[:, t * LANES:(t + 1) * LANES]))
            ss += [lo, hi]
        for u in range(2):
            y2 = jnp.square(yq[:, 512 + u * LANES:512 + (u + 1) * LANES])
            for g in range(4):
                ss[4 * u + g] = ss[4 * u + g] + jnp.sum(jnp.where(quad_id == g, y2, 0.0), axis=-1, keepdims=True)
        rq = [lax.rsqrt(x * (1.0 / width) + EPS) for x in ss]
        q_rope = []
        for u in range(2):
            sl = slice(512 + u * LANES, 512 + (u + 1) * LANES)
            x = yq[:, sl] * _quad_select(lane, rq[4 * u:4 * u + 4]) * gmq_r_ref[...]
            q_rope.append(rope(x).astype(BF16))
        for t in range(4):
            sl = slice(t * LANES, (t + 1) * LANES)
            r = jnp.where(lane < HEAD_DIM, rq[2 * t], rq[2 * t + 1])
            qm_ref[rows, 2 * t * LANES:(2 * t + 1) * LANES] = (yq[:, sl] * r * gmq_n_ref[:, sl]).astype(BF16)
            qm_ref[rows, (2 * t + 1) * LANES:(2 * t + 2) * LANES] = q_rope[t // 2]

        ckv = (_rms_rows(proj(_A_CKV, MLA_KV_RANK), MLA_KV_RANK) * gckv_ref[...]).astype(BF16)
        ykv = jnp.dot(ckv, wkvu_ref[...], preferred_element_type=F32)
        kr4 = proj(_A_KR, LANES)
        ss_rope = jnp.sum(kr4 * kr4, axis=-1, keepdims=True) * 0.25
        rk = []
        for t in range(4):
            lo, hi = _half_sums(jnp.square(ykv[:, t * LANES:(t + 1) * LANES]))
            rk += [lax.rsqrt((lo + ss_rope) * (1.0 / width) + EPS), lax.rsqrt((hi + ss_rope) * (1.0 / width) + EPS)]
        kr_base = rope(kr4 * gmk_r_ref[...])
        k_rope = [(kr_base * _quad_select(lane, rk[4 * u:4 * u + 4])).astype(BF16) for u in range(2)]
        for t in range(4):
            sl = slice(t * LANES, (t + 1) * LANES)
            r = jnp.where(lane < HEAD_DIM, rk[2 * t], rk[2 * t + 1])
            km_ref[rows, 2 * t * LANES:(2 * t + 1) * LANES] = (ykv[:, sl] * r * gmk_n_ref[:, sl]).astype(BF16)
            km_ref[rows, (2 * t + 1) * LANES:(2 * t + 2) * LANES] = k_rope[t // 2]
        store_vt(vtm_ref, wvtm_ref, ckv, rows)

    project(0)
    for s in range(tm // n):
        if s + 1 < tm // n:
            project(s + 1)
        finish(s)


def _in_proj(h, p, consts, tm):
    b, lp, d = h.shape
    nt = lp // tm
    row_blk = lambda w: pl.BlockSpec((None, tm, w), lambda bi, j: (bi, j, 0))
    full = lambda a: pl.BlockSpec(a.shape, lambda bi, j: (0,) * a.ndim)
    tab = pl.BlockSpec((tm, LANES), lambda bi, j: (j, 0))
    ins = [h, p['g1'], p['wa'], p['gfq'], p['gfk'], p['gsq'], p['gsk'], p['fb'],
           p['gcq'], p['wqu'], p['gckv'], p['wkvu'], p['gmq_n'], p['gmq_r'], p['gmk_n'], p['gmk_r'],
           p['wvtf'], p['wvtm'], p['wvts']]
    whole = [_whole(a) for a in ins[1:]]
    ins = [h] + [a for a, _ in whole]
    in_specs = [row_blk(d)] + [spec for _, spec in whole]
    ins += [consts['cos'], consts['sin'], consts['tri'], consts['eq'], consts['ek'], consts['qconst'], consts['kconst']]
    in_specs += [tab, tab] + [full(consts[k]) for k in ('tri', 'eq', 'ek', 'qconst', 'kconst')]
    outs = [('row', 1024), ('row', 1024), ('vt', N_HEADS), ('row', 1024), ('row', 1024), ('vt', N_HEADS),
            ('row', 512), ('row', 128), ('vt', SWA_KV_HEADS)]
    out_shape = [jax.ShapeDtypeStruct((b, lp, w) if kind == 'row' else (b, w * VT_ROWS, lp), BF16) for kind, w in outs]
    out_specs = [row_blk(w) if kind == 'row' else pl.BlockSpec((None, w * VT_ROWS, tm), lambda bi, j: (bi, 0, j))
                 for kind, w in outs]
    return pl.pallas_call(
        functools.partial(_in_proj_kernel, tm=tm),
        grid=(b, nt),
        in_specs=in_specs,
        out_specs=out_specs,
        out_shape=out_shape,
        scratch_shapes=[pltpu.VMEM((8, LANES), F32), pltpu.VMEM((tm, d), BF16),
                        pltpu.VMEM((2, IN_SUB, p['wa'][0].shape[-1]), F32)],
        compiler_params=pltpu.CompilerParams(dimension_semantics=("arbitrary", "arbitrary"),
                                             vmem_limit_bytes=VMEM_LIMIT),
        name="in_proj",
    )(*ins)


def _causal_attn_kernel(mask_ref, q_ref, k_ref, vt_ref, o_ref, m_ref, acc_ref, qs_ref, s_ref, *sd_refs, tq, n_blocks):
    masks = mask_ref[...]

    def init(n):
        m_ref[:, :, 0:n] = jnp.full((2, 1, n), NEG, F32)
        acc_ref[:, :, 0:n] = jnp.zeros((2, VT_ROWS, n), F32)

    def update(unit, st, vt, visible):
        a, c0, nc = unit
        if visible is not None:
            st = jnp.where(visible, st, NEG)
        cols = slice(c0, c0 + nc)
        m_prev = m_ref[a, :, cols]
        m_new = jnp.maximum(m_prev, jnp.max(st, axis=0, keepdims=True))
        alpha = jnp.exp2(m_prev - m_new)
        p = jnp.exp2(st - m_new).astype(BF16)
        acc_ref[a, :, cols] = alpha * acc_ref[a, :, cols] + jnp.dot(vt, p, preferred_element_type=F32)
        m_ref[a, :, cols] = m_new

    def finish(r0, n):
        ot = jnp.concatenate([acc_ref[a, 0:HEAD_DIM, 0:n] * (1.0 / acc_ref[a, HEAD_DIM:HEAD_DIM + 1, 0:n])
                              for a in range(2)], axis=0)
        o_ref[pl.ds(r0, n), :] = ot.T.astype(o_ref.dtype)

    def set_queries(qslot, r0, n):
        q = q_ref[pl.ds(r0, n), :]
        for a in range(2):
            qs_ref[qslot, a, 0:n, :] = q * masks[a:a + 1, :]

    def scores(qslot, unit, keys):
        a, c0, nc = unit
        return lax.dot_general(keys, qs_ref[qslot, a, c0:c0 + nc, :], _NT, preferred_element_type=F32)

    def values(a, k0, nk):
        return vt_ref[a * VT_ROWS:(a + 1) * VT_ROWS, pl.ds(k0, nk)]

    def visible(shape, c0, causal_lead):
        row = lax.broadcasted_iota(jnp.int32, shape, 0)
        col = lax.broadcasted_iota(jnp.int32, shape, 1)
        lead = row < FRONT
        lowest = jnp.where(lead, PAD, FRONT)
        highest = jnp.where(lead, col if causal_lead else FRONT, col + (c0 + FRONT))
        return (row >= lowest) & (row <= highest)

    init(FRONT)
    set_queries(0, 0, FRONT)
    lead_units = [(0, 0, FRONT), (1, 0, FRONT)]
    lead_keys = k_ref[0:FRONT, :]
    lead_sts = [scores(0, u, lead_keys) for u in lead_units]
    for u, st in zip(lead_units, lead_sts):
        update(u, st, values(u[0], 0, FRONT), visible(st.shape, 0, True))
    finish(0, FRONT)

    n_sub = tq // QUERY_SUB
    units = [(a, s * QUERY_SUB, QUERY_SUB) for a in range(2) for s in range(n_sub)]
    diag_keys = [FRONT + unit[1] + QUERY_SUB for unit in units]

    def block_start(j):
        return pl.multiple_of(FRONT + j * tq, LANES)

    def park_block(slot, qslot, j):
        keys = k_ref[pl.ds(block_start(j), tq), :]
        for u, unit in enumerate(units):
            s_ref[slot, u] = scores(qslot, unit, keys)

    def consume_block(slot, j):
        for u, unit in enumerate(units):
            update(unit, s_ref[slot, u], values(unit[0], block_start(j), tq), None)

    def park_diag(qslot, i):
        kk = jnp.concatenate([k_ref[0:FRONT, :], k_ref[pl.ds(block_start(i), tq), :]], axis=0)
        for u, (unit, nk) in enumerate(zip(units, diag_keys)):
            sd_refs[u][...] = scores(qslot, unit, kk[0:nk, :])

    def consume_diag(i):
        vts = [jnp.concatenate([values(a, 0, FRONT), values(a, block_start(i), tq)], axis=1) for a in range(2)]
        for u, (unit, nk) in enumerate(zip(units, diag_keys)):
            update(unit, sd_refs[u][...], vts[unit[0]][:, 0:nk], visible((nk, unit[2]), unit[1], False))

    def visible_pairs(qslot, i, n_pairs):
        def pair(t, c):
            park_block(1, qslot, 2 * t + 1)
            consume_block(0, 2 * t)
            park_block(0, qslot, jnp.minimum(2 * t + 2, i - 1))
            consume_block(1, 2 * t + 1)
            return c
        lax.fori_loop(0, n_pairs, pair, 0)

    set_queries(0, block_start(0), tq)

    def two_blocks(g, carry):
        even, odd = 2 * g, 2 * g + 1
        init(tq)
        visible_pairs(0, even, g)
        set_queries(1, block_start(odd), tq)
        park_diag(0, even)
        park_block(0, 1, 0)
        consume_diag(even)
        finish(block_start(even), tq)
        init(tq)
        visible_pairs(1, odd, g)
        park_diag(1, odd)
        consume_block(0, 2 * g)
        nxt = jnp.minimum(even + 2, n_blocks - 1)
        set_queries(0, block_start(nxt), tq)
        park_block(0, 0, 0)
        consume_diag(odd)
        finish(block_start(odd), tq)
        return carry

    lax.fori_loop(0, n_blocks // 2, two_blocks, 0)


def _causal_attn(q, k, vt, masks, tq):
    b, _, lp = vt.shape
    n_pairs = N_HEADS // 2
    n_blocks = (lp - FRONT) // tq
    assert tq % QUERY_SUB == 0 and n_blocks % 2 == 0
    n_sub = tq // QUERY_SUB
    n_units = 2 * n_sub

    def qk_spec():
        return pl.BlockSpec((None, lp, 2 * LANES), lambda bi, p: (bi, 0, p))

    return pl.pallas_call(
        functools.partial(_causal_attn_kernel, tq=tq, n_blocks=n_blocks),
        grid=(b, n_pairs),
        in_specs=[pl.BlockSpec((None, 2, 2 * LANES), lambda bi, p: (p, 0, 0)),
                  qk_spec(), qk_spec(),
                  pl.BlockSpec((None, 2 * VT_ROWS, lp), lambda bi, p: (bi, p, 0))],
        out_specs=pl.BlockSpec((None, lp, LANES), lambda bi, p: (bi, 0, p)),
        out_shape=jax.ShapeDtypeStruct((b, lp, BRANCH_WIDTH), BF16),
        scratch_shapes=[pltpu.VMEM((2, 1, tq), F32), pltpu.VMEM((2, VT_ROWS, tq), F32),
                        pltpu.VMEM((2, 2, tq, 2 * LANES), BF16),
                        pltpu.VMEM((2, n_units, tq, QUERY_SUB), F32),
                        *[pltpu.VMEM((FRONT + (u % n_sub + 1) * QUERY_SUB, QUERY_SUB), F32) for u in range(n_units)]],
        compiler_params=pltpu.CompilerParams(dimension_semantics=("arbitrary", "arbitrary"),
                                             vmem_limit_bytes=VMEM_LIMIT),
        name="causal_attn",
    )(masks, q, k, vt)


def _swa_kernel(sinks_ref, mask_ref, q_ref, qaux_ref, k_ref, kaux_ref, vt_ref, o_ref, bias_ref, s_ref, *, qb, n_blocks):
    masks = mask_ref[...]
    n_band = WINDOW + qb

    def key_bias(shape, n_band_rows, band_shift):
        row = lax.broadcasted_iota(jnp.int32, shape, 0)
        col = lax.broadcasted_iota(jnp.int32, shape, 1)
        band = row < n_band_rows
        lowest = jnp.where(band, col + (band_shift + 1), 0)
        highest = jnp.where(band, col + (band_shift + WINDOW), n_band_rows + N_META)
        return jnp.where((row >= lowest) & (row <= highest), 0.0, NEG)

    def keys(key_slices):
        return jnp.concatenate([jnp.concatenate([k_ref[pl.ds(s, n), :], kaux_ref[pl.ds(s, n), :]], axis=1)
                                for s, n in key_slices], axis=0)

    def values(key_slices):
        return [jnp.concatenate([vt_ref[g * VT_ROWS:(g + 1) * VT_ROWS, pl.ds(s, n)] for s, n in key_slices], axis=1)
                for g in range(SWA_KV_HEADS)]

    def tile_scores(r0, nq, t, kk):
        cols = slice(t * LANES, (t + 1) * LANES)
        qt = jnp.concatenate([q_ref[pl.ds(r0, nq), cols], qaux_ref[pl.ds(r0, nq), cols]], axis=1)
        return [lax.dot_general(kk, qt * masks[a:a + 1, :], _NT, preferred_element_type=F32) for a in range(2)]

    def tile_finish(r0, nq, t, sts, vts, bias):
        outs = []
        for a in range(2):
            sink = sinks_ref[t + 4 * a]
            st = sts[a] + bias
            m = jnp.maximum(jnp.max(st, axis=0, keepdims=True), sink)
            p = jnp.exp(st - m).astype(BF16)
            acc = jnp.dot(vts[a], p, preferred_element_type=F32)
            den = acc[HEAD_DIM:HEAD_DIM + 1, :] + jnp.exp(sink - m)
            outs.append(acc[0:HEAD_DIM, :] * (1.0 / den))
        o_ref[pl.ds(r0, nq), t * LANES:(t + 1) * LANES] = jnp.concatenate(outs, axis=0).T.astype(o_ref.dtype)

    def attend(r0, nq, key_slices, bias):
        kk, vts = keys(key_slices), values(key_slices)
        for t in range(4):
            tile_finish(r0, nq, t, tile_scores(r0, nq, t, kk), vts, bias)

    row = lax.broadcasted_iota(jnp.int32, (FRONT, FRONT), 0)
    col = lax.broadcasted_iota(jnp.int32, (FRONT, FRONT), 1)
    attend(0, FRONT, [(0, FRONT)], jnp.where((row >= PAD) & (row <= col), 0.0, NEG))
    meta = (PAD, N_META)
    attend(FRONT, qb, [(FRONT, qb), meta], key_bias((qb + N_META, qb), qb, -WINDOW))
    bias_ref[...] = key_bias(bias_ref.shape, n_band, 0)

    def block_keys(i):
        return [(pl.multiple_of(FRONT + i * qb - WINDOW, LANES), n_band), meta]

    def park_scores(slot, i, t):
        sts = tile_scores(pl.multiple_of(FRONT + i * qb, LANES), qb, t, keys(block_keys(i)))
        for a in range(2):
            s_ref[slot, a] = sts[a]

    park_scores(0, 1, 0)

    def q_block(i, carry):
        r0 = pl.multiple_of(FRONT + i * qb, LANES)
        vts = values(block_keys(i))
        for t in range(4):
            if t < 3:
                park_scores((t + 1) % 2, i, t + 1)
            else:
                park_scores(0, jnp.minimum(i + 1, n_blocks - 1), 0)
            tile_finish(r0, qb, t, [s_ref[t % 2, a] for a in range(2)], vts, bias_ref[...])
        return carry

    lax.fori_loop(1, n_blocks, q_block, 0)


def _swa_attn(q, k, vt, sinks, consts):
    b, lp, _ = q.shape
    qb = SWA_QUERY_BLOCK
    assert (lp - FRONT) % qb == 0
    full = lambda a: pl.BlockSpec(a.shape, lambda bi: (0,) * a.ndim)
    per_batch = lambda r, c: pl.BlockSpec((None, r, c), lambda bi: (bi, 0, 0))
    return pl.pallas_call(
        functools.partial(_swa_kernel, qb=qb, n_blocks=(lp - FRONT) // qb),
        grid=(b,),
        in_specs=[pl.BlockSpec(memory_space=pltpu.SMEM), full(consts['swa_masks']),
                  per_batch(lp, BRANCH_WIDTH), full(consts['swa_qaux']),
                  per_batch(lp, LANES), full(consts['swa_kaux']),
                  per_batch(SWA_KV_HEADS * VT_ROWS, lp)],
        out_specs=per_batch(lp, BRANCH_WIDTH),
        out_shape=jax.ShapeDtypeStruct((b, lp, BRANCH_WIDTH), BF16),
        scratch_shapes=[pltpu.VMEM((WINDOW + qb + N_META, qb), F32),
                        pltpu.VMEM((2, 2, WINDOW + qb + N_META, qb), F32)],
        compiler_params=pltpu.CompilerParams(dimension_semantics=("arbitrary",), vmem_limit_bytes=VMEM_LIMIT),
        name="swa_attn",
    )(sinks, consts['swa_masks'], q, consts['swa_qaux'], k, consts['swa_kaux'], vt)


def _merge_kernel(h_ref, oa_ref, ob_ref, oc_ref, g1_ref, wg_ref, wb_ref, wo_ref, out_ref):
    h = h_ref[...]
    d = h.shape[-1]
    xn = (_rms_rows(h, d) * g1_ref[...]).astype(BF16)
    merged = None
    for n, o_ref in enumerate((oa_ref, ob_ref, oc_ref)):
        gate = jax.nn.sigmoid(jnp.dot(xn, wg_ref[:, n * d:(n + 1) * d], preferred_element_type=F32))
        y = jnp.dot(o_ref[...], wb_ref[n], preferred_element_type=F32)
        merged = gate * y if merged is None else merged + gate * y
    out_ref[...] = h + jnp.dot(merged.astype(BF16), wo_ref[...], preferred_element_type=F32)


def _merge(h, oa, ob, oc, p, tm):
    b, lp, d = h.shape
    row_blk = lambda w: pl.BlockSpec((None, tm, w), lambda bi, j: (bi, j, 0))
    whole = [_whole(p[k], pipeline_mode=pl.Buffered(1)) for k in ('g1', 'wg', 'wb', 'wo')]
    ws = [a for a, _ in whole]
    return pl.pallas_call(
        _merge_kernel,
        grid=(b, lp // tm),
        in_specs=[row_blk(d), row_blk(BRANCH_WIDTH), row_blk(BRANCH_WIDTH), row_blk(BRANCH_WIDTH)]
        + [spec for _, spec in whole],
        out_specs=row_blk(d),
        out_shape=jax.ShapeDtypeStruct(h.shape, F32),
        compiler_params=pltpu.CompilerParams(dimension_semantics=("arbitrary", "arbitrary"),
                                             vmem_limit_bytes=VMEM_LIMIT),
        name="merge",
    )(h, oa, ob, oc, *ws)


_FF_CHUNK = 256
_HALO = 8


def _ffn_kernel(h_ref, prev_ref, g2_ref, wup_ref, cw_ref, cb_ref, wdn_ref, out_ref, carry_ref, ubuf_ref, act_ref, *,
                tm, d_ff, real_rows_only):
    j = pl.program_id(1)
    h = h_ref[...]
    xn = _rms_rows(h, h.shape[-1]) * g2_ref[...]
    if not real_rows_only:
        row = j * tm + lax.broadcasted_iota(jnp.int32, (tm, 1), 0)
        xn = jnp.where(row >= PAD, xn, 0.0)
    xn = xn.astype(BF16)

    @pl.when(j == 0)
    def _():
        if real_rows_only:
            prev = prev_ref[...]
            xp = (_rms_rows(prev, prev.shape[-1]) * g2_ref[...]).astype(BF16)
            u_prev = jnp.dot(xp, wup_ref[...], preferred_element_type=F32)
            carry_ref[...] = u_prev[N_META - _HALO:N_META, :]
        else:
            carry_ref[...] = jnp.zeros_like(carry_ref)

    for c in range(d_ff // _FF_CHUNK):
        acts = []
        for half in range(2):
            c0 = half * d_ff + c * _FF_CHUNK
            cols = slice(c0, c0 + _FF_CHUNK)
            u = jnp.dot(xn, wup_ref[:, cols], preferred_element_type=F32)
            ubuf_ref[half, 0:_HALO, :] = carry_ref[:, cols]
            ubuf_ref[half, _HALO:_HALO + tm, :] = u
            carry_ref[:, cols] = u[tm - _HALO:tm, :]
            u1 = ubuf_ref[half, _HALO - 1:_HALO - 1 + tm, :]
            u2 = ubuf_ref[half, _HALO - 2:_HALO - 2 + tm, :]
            acts.append(cb_ref[:, cols] + cw_ref[2:3, cols] * u + cw_ref[1:2, cols] * u1 + cw_ref[0:1, cols] * u2)
        gate, val = acts
        act_ref[:, c * _FF_CHUNK:(c + 1) * _FF_CHUNK] = (gate * jax.nn.sigmoid(gate) * val).astype(BF16)
    out_ref[...] = h + jnp.dot(act_ref[...], wdn_ref[...], preferred_element_type=F32)


def _ffn(h, p, tm, real_rows_only=False):
    b, lp, d = h.shape
    row_blk = pl.BlockSpec((None, tm, d), lambda bi, j: (bi, j, 0))
    n_rows = lp - FRONT if real_rows_only else lp
    if real_rows_only:
        h_spec = pl.BlockSpec((None, pl.Element(tm), pl.Element(d)),
                              lambda bi, j: (bi, pl.multiple_of(FRONT + j * tm, LANES), 0))
        prev_spec = pl.BlockSpec((None, pl.Element(N_META), pl.Element(d)), lambda bi, j: (bi, PAD, 0))
    else:
        h_spec = row_blk
        prev_spec = pl.BlockSpec((None, N_META, d), lambda bi, j: (bi, 0, 0))
    whole = [_whole(p[k], pipeline_mode=pl.Buffered(1)) for k in ('g2', 'wup', 'cw', 'cb', 'wdn')]
    ws = [a for a, _ in whole]
    d_ff = ws[-1].shape[-2]
    return pl.pallas_call(
        functools.partial(_ffn_kernel, tm=tm, d_ff=d_ff, real_rows_only=real_rows_only),
        grid=(b, n_rows // tm),
        in_specs=[h_spec, prev_spec] + [spec for _, spec in whole],
        out_specs=row_blk,
        out_shape=jax.ShapeDtypeStruct((b, n_rows, d), F32),
        scratch_shapes=[pltpu.VMEM((_HALO, 2 * d_ff), F32),
                        pltpu.VMEM((2, _HALO + tm, _FF_CHUNK), F32),
                        pltpu.VMEM((tm, d_ff), BF16)],
        compiler_params=pltpu.CompilerParams(dimension_semantics=("arbitrary", "arbitrary"),
                                             vmem_limit_bytes=VMEM_LIMIT),
        name="conv_ffn",
    )(h, h, *ws)


def _constants(lp, tm):
    half = ROPE_DIM // 2
    freqs = ROPE_THETA ** (-np.arange(half, dtype=np.float64) / half)
    lane = np.arange(LANES)
    ang = (np.arange(lp) - PAD).astype(np.float64)[:, None] * freqs[lane % half][None, :]
    sign = np.where((lane % ROPE_DIM) < half, -1.0, 1.0)
    row = np.arange(lp)
    pos_hi, pos_lo = row // LANES, row % LANES
    swa_kaux = np.zeros((lp, LANES), np.float32)
    swa_qaux = np.zeros((lp, BRANCH_WIDTH), np.float32)
    swa_masks = np.zeros((2, 2 * LANES), np.float32)
    for a in range(2):
        swa_masks[a, a * HEAD_DIM:(a + 1) * HEAD_DIM] = 1.0
        swa_masks[a, LANES + 4 * a:LANES + 4 * (a + 1)] = 1.0
        swa_kaux[:, 4 * a:4 * (a + 1)] = np.stack([np.ones(lp), np.ones(lp), pos_hi, pos_lo], axis=1)
        for t in range(4):
            slope = 2.0 ** (-8.0 * (t + 4 * a + 1) / N_HEADS)
            swa_qaux[:, t * LANES + 4 * a:t * LANES + 4 * (a + 1)] = np.stack(
                [-LANES * slope * pos_hi, -slope * pos_lo, np.full(lp, LANES * slope), np.full(lp, slope)], axis=1)
    tri = np.tril(np.ones((tm, tm), np.float32))
    eq = np.zeros((DEC_PARTS, LANES, LANES), np.float32)
    ek = np.zeros((DEC_PARTS, LANES, LANES), np.float32)
    qconst = np.zeros((1, LANES), np.float32)
    kconst = np.zeros((1, LANES), np.float32)
    for hd in range(N_HEADS):
        base = 2 * DEC_PARTS * hd
        for i in range(DEC_PARTS):
            eq[i, hd, base + i] = 1.0
            ek[i, hd, base + DEC_PARTS + i] = -1.0
            qconst[0, base + DEC_PARTS + i] = 1.0
            kconst[0, base + i] = 1.0
    fox_masks = np.zeros((N_HEADS // 2, 2, 2 * LANES), np.float32)
    mla_masks = np.zeros((N_HEADS // 2, 2, 2 * LANES), np.float32)
    for p in range(N_HEADS // 2):
        for a in range(2):
            hd = 2 * p + a
            fox_masks[p, a, a * HEAD_DIM:(a + 1) * HEAD_DIM] = 1.0
            fox_masks[p, a, LANES + 2 * DEC_PARTS * hd:LANES + 2 * DEC_PARTS * (hd + 1)] = 1.0
            mla_masks[p, a, a * HEAD_DIM:(a + 1) * HEAD_DIM] = 1.0
            g = hd % 4
            mla_masks[p, a, LANES + g * ROPE_DIM:LANES + (g + 1) * ROPE_DIM] = 1.0
    return {
        'cos': jnp.asarray(np.cos(ang), F32), 'sin': jnp.asarray(np.sin(ang) * sign[None, :], F32),
        'tri': jnp.asarray(tri, BF16), 'eq': jnp.asarray(eq, BF16), 'ek': jnp.asarray(ek, BF16),
        'qconst': jnp.asarray(qconst), 'kconst': jnp.asarray(kconst),
        'fox_masks': jnp.asarray(fox_masks, BF16), 'mla_masks': jnp.asarray(mla_masks, BF16),
        'swa_masks': jnp.asarray(swa_masks, BF16), 'swa_qaux': jnp.asarray(swa_qaux, BF16),
        'swa_kaux': jnp.asarray(swa_kaux, BF16),
    }


def _swa_head_order(a, axis):
    shape = a.shape
    a = a.reshape(shape[:axis] + (SWA_KV_HEADS, N_HEADS // SWA_KV_HEADS, HEAD_DIM) + shape[axis + 1:])
    return jnp.swapaxes(a, axis, axis + 1).reshape(shape)


_O_FQ, _O_FK, _O_FV, _O_FF = 0, 512, 1024, 1536
_O_CQ = _O_FF + N_HEADS
_O_CKV = _O_CQ + MLA_Q_RANK
_O_KR = _O_CKV + MLA_KV_RANK
_O_SQ = _O_KR + ROPE_DIM
_O_SK = _O_SQ + 512
_O_SV = _O_SK + SWA_KV_HEADS * HEAD_DIM
_O_GATES = _O_SV + SWA_KV_HEADS * HEAD_DIM


def _stacked_weights(w_in, w_branch, w_o, ffn_w_up, ffn_w_down):
    n_layers, d, _ = w_in.shape
    col = lambda c0, n: w_in[:, :, c0:c0 + n]
    wa = jnp.concatenate([
        col(_O_FQ, 512), col(_O_FK, 512), _swa_head_order(col(_O_SQ, 512), 2), col(_O_SK, 128),
        col(_O_CQ, MLA_Q_RANK), col(_O_CKV, MLA_KV_RANK),
        jnp.tile(col(_O_KR, ROPE_DIM), (1, 1, LANES // ROPE_DIM)),
        col(_O_FF, N_HEADS), jnp.zeros((n_layers, d, LANES - N_HEADS), w_in.dtype)], axis=2).astype(BF16)
    wb = jnp.concatenate([w_branch[:, :2], _swa_head_order(w_branch[:, 2:3], 2)], axis=1).astype(BF16)
    return {'wa': wa, 'wg': w_in[:, :, _O_GATES:].astype(BF16), 'wb': wb, 'wo': w_o.astype(BF16),
            'wup': ffn_w_up.astype(BF16), 'wdn': ffn_w_down.astype(BF16)}


def _layer_params(l, stacked, norm1_g, w_in, fox_forget_b, fox_q_g, fox_k_g, mla_q_a_g, mla_w_q_up, mla_kv_a_g,
                  mla_w_kv_up, mla_q_g, mla_k_g, swa_q_g, swa_k_g, swa_sinks, w_branch, w_o,
                  norm2_g, ffn_w_up, ffn_conv_w, ffn_conv_b, ffn_w_down):
    w = w_in[l]
    tile_row = lambda g, n: jnp.tile(g, n)[None, :].astype(F32)
    width = HEAD_DIM + ROPE_DIM
    qcols = np.concatenate([np.concatenate([np.arange(hd * width, hd * width + HEAD_DIM) for hd in range(N_HEADS)]),
                            np.concatenate([np.arange(hd * width + HEAD_DIM, (hd + 1) * width) for hd in range(N_HEADS)])])
    kcols = np.concatenate([np.arange(hd * 2 * HEAD_DIM, hd * 2 * HEAD_DIM + HEAD_DIM) for hd in range(N_HEADS)])
    vcols = kcols + HEAD_DIM
    fb = jnp.concatenate([fox_forget_b[l], jnp.zeros((LANES - N_HEADS,), F32)])[None, :]
    return {
        'g1': norm1_g[l][None, :], 'wa': (stacked['wa'], l),
        'gfq': tile_row(fox_q_g[l], N_HEADS) * (LOG2E * HEAD_DIM ** -0.5), 'gfk': tile_row(fox_k_g[l], N_HEADS),
        'gsq': tile_row(swa_q_g[l], N_HEADS) * (HEAD_DIM ** -0.5), 'gsk': tile_row(swa_k_g[l], SWA_KV_HEADS),
        'fb': fb,
        'gcq': mla_q_a_g[l][None, :], 'wqu': mla_w_q_up[l][:, qcols].astype(BF16),
        'gckv': mla_kv_a_g[l][None, :], 'wkvu': mla_w_kv_up[l][:, kcols].astype(BF16),
        'wvtf': w[:, _O_FV:_O_FV + 512].T.astype(BF16), 'wvtm': mla_w_kv_up[l][:, vcols].T.astype(BF16),
        'wvts': w[:, _O_SV:_O_SV + SWA_KV_HEADS * HEAD_DIM].T.astype(BF16),
        'gmq_n': tile_row(mla_q_g[l][:HEAD_DIM], N_HEADS) * (LOG2E * width ** -0.5),
        'gmq_r': tile_row(mla_q_g[l][HEAD_DIM:], LANES // ROPE_DIM) * (LOG2E * width ** -0.5),
        'gmk_n': tile_row(mla_k_g[l][:HEAD_DIM], N_HEADS),
        'gmk_r': tile_row(mla_k_g[l][HEAD_DIM:], LANES // ROPE_DIM),
        'sinks': swa_sinks[l].astype(F32),
        'wg': (stacked['wg'], l), 'wb': (stacked['wb'], l), 'wo': (stacked['wo'], l),
        'g2': norm2_g[l][None, :], 'wup': (stacked['wup'], l), 'cw': ffn_conv_w[l], 'cb': ffn_conv_b[l][None, :],
        'wdn': (stacked['wdn'], l),
    }


def kernel(x, meta_tokens, norm1_g, w_in, fox_forget_b, fox_q_g, fox_k_g, mla_q_a_g, mla_w_q_up, mla_kv_a_g, mla_w_kv_up, mla_q_g, mla_k_g, swa_q_g, swa_k_g, swa_sinks, w_branch, w_o, norm2_g, ffn_w_up, ffn_conv_w, ffn_conv_b, ffn_w_down):
    b, seq, d = x.shape
    assert seq % LANES == 0 and meta_tokens.shape == (N_META, d)
    lp = FRONT + seq
    tm = _row_tile(lp)
    tq = CAUSAL_BLOCK if seq % (2 * CAUSAL_BLOCK) == 0 else _row_tile(seq)
    consts = _constants(lp, IN_SUB)
    front = jnp.concatenate([jnp.zeros((PAD, d), x.dtype), meta_tokens.astype(x.dtype)], axis=0)
    h = jnp.concatenate([jnp.broadcast_to(front[None], (b, FRONT, d)), x], axis=1)
    weights = (norm1_g, w_in, fox_forget_b, fox_q_g, fox_k_g, mla_q_a_g, mla_w_q_up, mla_kv_a_g, mla_w_kv_up,
               mla_q_g, mla_k_g, swa_q_g, swa_k_g, swa_sinks, w_branch, w_o, norm2_g, ffn_w_up, ffn_conv_w,
               ffn_conv_b, ffn_w_down)
    stacked = _stacked_weights(w_in, w_branch, w_o, ffn_w_up, ffn_w_down)
    for l in range(w_in.shape[0]):
        p = _layer_params(l, stacked, *weights)
        qf, kf, vtf, qm, km, vtm, sq, sk, vts = _in_proj(h, p, consts, tm)
        out_a = _causal_attn(qf, kf, vtf, consts['fox_masks'], tq)
        out_b = _causal_attn(qm, km, vtm, consts['mla_masks'], tq)
        out_c = _swa_attn(sq, sk, vts, p['sinks'], consts)
        h = _merge(h, out_a, out_b, out_c, p, tm)
        last = l == w_in.shape[0] - 1
        h = _ffn(h, p, _row_tile(seq) if last else tm, real_rows_only=last)
    return h
```

```python
import functools
import math

import numpy as np
import jax
import jax.numpy as jnp
from jax import lax
from jax.experimental import pallas as pl
from jax.experimental.pallas import tpu as pltpu

F32 = jnp.float32
BF16 = jnp.bfloat16

N_META = 16
EPS = 1e-6
N_HEADS = 8
HEAD_DIM = 64
ROPE_DIM = 32
MLA_Q_RANK = 256
MLA_KV_RANK = 128
SWA_KV_HEADS = 2
WINDOW = 128
ROPE_THETA = 10000.0
BRANCH_WIDTH = N_HEADS * HEAD_DIM
N_BRANCH = 3
CONV_WIDTH = 3

LANES = 128
FRONT = 128
PAD = FRONT - N_META
NEG = -1e30
DEC_PARTS = 3
LOG2E = math.log2(math.e)
ONES_ROWS = 16
VT_ROWS = HEAD_DIM + ONES_ROWS
QUERY_SUB = 256
CAUSAL_BLOCK = 1024
IN_SUB = 128
SWA_QUERY_BLOCK = 256
VMEM_LIMIT = 56 * 1024 * 1024

_A_FQ, _A_FK, _A_SQ, _A_SK, _A_CQ, _A_CKV, _A_KR, _A_FF = 0, 512, 1024, 1536, 1664, 1920, 2048, 2176
_NT = (((1,), (1,)), ((), ()))


def _row_tile(lp):
    best = 128
    for t in range(128, 513, 128):
        if lp % t == 0:
            best = t
    return best


def _whole(w, **kw):
    if isinstance(w, tuple):
        arr, layer = w
        return arr, pl.BlockSpec((None,) + arr.shape[1:], lambda *g: (layer,) + (0,) * (arr.ndim - 1), **kw)
    return w, pl.BlockSpec(w.shape, lambda *g: (0,) * w.ndim, **kw)


def _stream_spec(tm, d, from_x):
    if not from_x:
        return pl.BlockSpec((None, tm, d), lambda bi, j: (bi, j, 0))
    return pl.BlockSpec((None, pl.Element(tm), pl.Element(d)),
                        lambda bi, j: (bi, pl.multiple_of(jnp.maximum(j * tm - FRONT, 0), LANES), 0))


def _stream_rows(h_ref, front_ref, j, r0, n, from_x):
    if not from_x:
        return h_ref[r0:r0 + n, :]
    lead = [front_ref[r0:min(r0 + n, FRONT), :]] if r0 < FRONT else []
    rest = [h_ref[max(r0 - FRONT, 0):r0 + n - FRONT, :]] if r0 + n > FRONT else []
    first_tile = jnp.concatenate(lead + rest, axis=0)
    return jnp.where(j == 0, first_tile, h_ref[r0:r0 + n, :])


def _lane_iota(shape):
    return lax.broadcasted_iota(jnp.int32, shape, len(shape) - 1)


def _rms_rows(y, width):
    ss = jnp.sum(y * y, axis=-1, keepdims=True)
    return y * lax.rsqrt(ss * (1.0 / width) + EPS)


def _half_sums(y2):
    lane = _lane_iota(y2.shape)
    lo = jnp.sum(jnp.where(lane < HEAD_DIM, y2, 0.0), axis=-1, keepdims=True)
    hi = jnp.sum(jnp.where(lane >= HEAD_DIM, y2, 0.0), axis=-1, keepdims=True)
    return lo, hi


def _headnorm64_tile(yt):
    lo, hi = _half_sums(yt * yt)
    lane = _lane_iota(yt.shape)
    r = jnp.where(lane < HEAD_DIM, lax.rsqrt(lo * (1.0 / HEAD_DIM) + EPS), lax.rsqrt(hi * (1.0 / HEAD_DIM) + EPS))
    return yt * r


def _rotate_half(x):
    lane = _lane_iota(x.shape)
    half = ROPE_DIM // 2
    return jnp.where((lane % ROPE_DIM) < half, pltpu.roll(x, LANES - half, 1), pltpu.roll(x, half, 1))


def _quad_select(lane, vals):
    g = lane // ROPE_DIM
    out = jnp.where(g == 0, vals[0], vals[1])
    out = jnp.where(g == 2, vals[2], out)
    return jnp.where(g == 3, vals[3], out)


def _in_proj_kernel(h_ref, front_ref, g1_ref, wa_ref, gfq_ref, gfk_ref, gsq_ref, gsk_ref, fb_ref,
                    gcq_ref, wqu_ref, gckv_ref, wkvu_ref, gmq_n_ref, gmq_r_ref, gmk_n_ref, gmk_r_ref,
                    wvtf_ref, wvtm_ref, wvts_ref,
                    cos_ref, sin_ref, tri_ref, eq_ref, ek_ref, qconst_ref, kconst_ref,
                    qf_ref, kf_ref, vtf_ref, qm_ref, km_ref, vtm_ref, sq_ref, sk_ref, vts_ref,
                    carry_ref, xn_ref, y_ref, *, tm, from_x):
    j = pl.program_id(1)

    def store_vt(vt_ref, wt_ref, x, cols):
        yt = lax.dot_general(wt_ref[...], x, _NT, preferred_element_type=F32)
        ones = jnp.ones((ONES_ROWS, yt.shape[1]), BF16)
        for hd in range(yt.shape[0] // HEAD_DIM):
            vt_ref[hd * VT_ROWS:hd * VT_ROWS + HEAD_DIM, cols] = yt[hd * HEAD_DIM:(hd + 1) * HEAD_DIM, :].astype(BF16)
            vt_ref[hd * VT_ROWS + HEAD_DIM:(hd + 1) * VT_ROWS, cols] = ones

    @pl.when(j == 0)
    def _():
        carry_ref[...] = jnp.zeros_like(carry_ref)

    n = IN_SUB
    lane = _lane_iota((n, LANES))
    quad_id = lane // ROPE_DIM
    width = HEAD_DIM + ROPE_DIM

    def project(s):
        h = _stream_rows(h_ref, front_ref, j, s * n, n, from_x)
        xn = (_rms_rows(h, h.shape[-1]) * g1_ref[...]).astype(BF16)
        xn_ref[s * n:(s + 1) * n, :] = xn
        y_ref[s % 2] = jnp.dot(xn, wa_ref[...], preferred_element_type=F32)

    def finish(s):
        rows = slice(s * n, (s + 1) * n)

        def proj(c0, w):
            return y_ref[s % 2, :, c0:c0 + w]

        yq = proj(_A_FQ, 512)
        yk = proj(_A_FK, 512)
        for t in range(4):
            sl = slice(t * LANES, (t + 1) * LANES)
            qf_ref[rows, 2 * t * LANES:(2 * t + 1) * LANES] = (_headnorm64_tile(yq[:, sl]) * gfq_ref[:, sl]).astype(BF16)
            kf_ref[rows, 2 * t * LANES:(2 * t + 1) * LANES] = (_headnorm64_tile(yk[:, sl]) * gfk_ref[:, sl]).astype(BF16)

        z = proj(_A_FF, LANES) + fb_ref[...]
        ls = jnp.minimum(z, 0.0) - jnp.log1p(jnp.exp(-jnp.abs(z)))
        row = j * tm + s * n + lax.broadcasted_iota(jnp.int32, ls.shape, 0)
        ls = jnp.where((lane < N_HEADS) & (row >= PAD), ls * LOG2E, 0.0)
        tri = tri_ref[...]
        c = carry_ref[0:1, :]
        rem = ls
        for _ in range(DEC_PARTS):
            part = rem.astype(BF16)
            c = c + jnp.dot(tri, part, preferred_element_type=F32)
            rem = rem - part.astype(F32)
        carry_ref[0:1, :] = c[n - 1:n, :]
        qd = qconst_ref[...]
        kd = kconst_ref[...]
        rem = c
        for i in range(DEC_PARTS):
            part = rem.astype(BF16)
            qd = qd + jnp.dot(part, eq_ref[i], preferred_element_type=F32)
            kd = kd + jnp.dot(part, ek_ref[i], preferred_element_type=F32)
            rem = rem - part.astype(F32)
        qd = qd.astype(BF16)
        kd = kd.astype(BF16)
        for t in range(4):
            qf_ref[rows, (2 * t + 1) * LANES:(2 * t + 2) * LANES] = qd
            kf_ref[rows, (2 * t + 1) * LANES:(2 * t + 2) * LANES] = kd

        ysq = proj(_A_SQ, 512)
        for t in range(4):
            sl = slice(t * LANES, (t + 1) * LANES)
            sq_ref[rows, sl] = (_headnorm64_tile(ysq[:, sl]) * gsq_ref[:, sl]).astype(BF16)
        sk_ref[rows, :] = (_headnorm64_tile(proj(_A_SK, LANES)) * gsk_ref[...]).astype(BF16)

        cos = cos_ref[rows, :]
        sin = sin_ref[rows, :]

        def rope(x):
            return x * cos + _rotate_half(x) * sin

        cq = (_rms_rows(proj(_A_CQ, MLA_Q_RANK), MLA_Q_RANK) * gcq_ref[...]).astype(BF16)
        yq = jnp.dot(cq, wqu_ref[...], preferred_element_type=F32)
        ss = []
        for t in range(4):
            lo, hi = _half_sums(jnp.square(yq[:, t * LANES:(t + 1) * LANES]))
            ss += [lo, hi]
        for u in range(2):
            y2 = jnp.square(yq[:, 512 + u * LANES:512 + (u + 1) * LANES])
            for g in range(4):
                ss[4 * u + g] = ss[4 * u + g] + jnp.sum(jnp.where(quad_id == g, y2, 0.0), axis=-1, keepdims=True)
        rq = [lax.rsqrt(x * (1.0 / width) + EPS) for x in ss]
        q_rope = []
        for u in range(2):
            sl = slice(512 + u * LANES, 512 + (u + 1) * LANES)
            x = yq[:, sl] * _quad_select(lane, rq[4 * u:4 * u + 4]) * gmq_r_ref[...]
            q_rope.append(rope(x).astype(BF16))
        for t in range(4):
            sl = slice(t * LANES, (t + 1) * LANES)
            r = jnp.where(lane < HEAD_DIM, rq[2 * t], rq[2 * t + 1])
            qm_ref[rows, 2 * t * LANES:(2 * t + 1) * LANES] = (yq[:, sl] * r * gmq_n_ref[:, sl]).astype(BF16)
            qm_ref[rows, (2 * t + 1) * LANES:(2 * t + 2) * LANES] = q_rope[t // 2]

        ckv = (_rms_rows(proj(_A_CKV, MLA_KV_RANK), MLA_KV_RANK) * gckv_ref[...]).astype(BF16)
        ykv = jnp.dot(ckv, wkvu_ref[...], preferred_element_type=F32)
        kr4 = proj(_A_KR, LANES)
        ss_rope = jnp.sum(kr4 * kr4, axis=-1, keepdims=True) * 0.25
        rk = []
        for t in range(4):
            lo, hi = _half_sums(jnp.square(ykv[:, t * LANES:(t + 1) * LANES]))
            rk += [lax.rsqrt((lo + ss_rope) * (1.0 / width) + EPS), lax.rsqrt((hi + ss_rope) * (1.0 / width) + EPS)]
        kr_base = rope(kr4 * gmk_r_ref[...])
        k_rope = [(kr_base * _quad_select(lane, rk[4 * u:4 * u + 4])).astype(BF16) for u in range(2)]
        for t in range(4):
            sl = slice(t * LANES, (t + 1) * LANES)
            r = jnp.where(lane < HEAD_DIM, rk[2 * t], rk[2 * t + 1])
            km_ref[rows, 2 * t * LANES:(2 * t + 1) * LANES] = (ykv[:, sl] * r * gmk_n_ref[:, sl]).astype(BF16)
            km_ref[rows, (2 * t + 1) * LANES:(2 * t + 2) * LANES] = k_rope[t // 2]
        store_vt(vtm_ref, wvtm_ref, ckv, rows)

    project(0)
    for s in range(tm // n):
        if s + 1 < tm // n:
            project(s + 1)
        else:
            store_vt(vtf_ref, wvtf_ref, xn_ref[...], slice(0, tm))
            store_vt(vts_ref, wvts_ref, xn_ref[...], slice(0, tm))
        finish(s)


def _in_proj(h, front, p, consts, tm, from_x):
    b, _, d = h.shape
    lp = consts['cos'].shape[0]
    nt = lp // tm
    row_blk = lambda w: pl.BlockSpec((None, tm, w), lambda bi, j: (bi, j, 0))
    full = lambda a: pl.BlockSpec(a.shape, lambda bi, j: (0,) * a.ndim)
    tab = pl.BlockSpec((tm, LANES), lambda bi, j: (j, 0))
    ins = [h, front, p['g1'], p['wa'], p['gfq'], p['gfk'], p['gsq'], p['gsk'], p['fb'],
           p['gcq'], p['wqu'], p['gckv'], p['wkvu'], p['gmq_n'], p['gmq_r'], p['gmk_n'], p['gmk_r'],
           p['wvtf'], p['wvtm'], p['wvts']]
    whole = [_whole(a) for a in ins[1:]]
    ins = [h] + [a for a, _ in whole]
    in_specs = [_stream_spec(tm, d, from_x)] + [spec for _, spec in whole]
    ins += [consts['cos'], consts['sin'], consts['tri'], consts['eq'], consts['ek'], consts['qconst'], consts['kconst']]
    in_specs += [tab, tab] + [full(consts[k]) for k in ('tri', 'eq', 'ek', 'qconst', 'kconst')]
    outs = [('row', 1024), ('row', 1024), ('vt', N_HEADS), ('row', 1024), ('row', 1024), ('vt', N_HEADS),
            ('row', 512), ('row', 128), ('vt', SWA_KV_HEADS)]
    out_shape = [jax.ShapeDtypeStruct((b, lp, w) if kind == 'row' else (b, w * VT_ROWS, lp), BF16) for kind, w in outs]
    out_specs = [row_blk(w) if kind == 'row' else pl.BlockSpec((None, w * VT_ROWS, tm), lambda bi, j: (bi, 0, j))
                 for kind, w in outs]
    return pl.pallas_call(
        functools.partial(_in_proj_kernel, tm=tm, from_x=from_x),
        grid=(b, nt),
        in_specs=in_specs,
        out_specs=out_specs,
        out_shape=out_shape,
        scratch_shapes=[pltpu.VMEM((8, LANES), F32), pltpu.VMEM((tm, d), BF16),
                        pltpu.VMEM((2, IN_SUB, p['wa'][0].shape[-1]), F32)],
        compiler_params=pltpu.CompilerParams(dimension_semantics=("arbitrary", "arbitrary"),
                                             vmem_limit_bytes=VMEM_LIMIT),
        name="in_proj",
    )(*ins)


def _causal_attn_kernel(mask_ref, q_ref, k_ref, vt_ref, o_ref, m_ref, acc_ref, qs_ref, s_ref, *sd_refs, tq, n_blocks):
    masks = mask_ref[...]

    def init(n):
        m_ref[:, :, 0:n] = jnp.full((2, 1, n), NEG, F32)
        acc_ref[:, :, 0:n] = jnp.zeros((2, VT_ROWS, n), F32)

    def update(unit, st, vt, visible):
        a, c0, nc = unit
        if visible is not None:
            st = jnp.where(visible, st, NEG)
        cols = slice(c0, c0 + nc)
        m_prev = m_ref[a, :, cols]
        m_new = jnp.maximum(m_prev, jnp.max(st, axis=0, keepdims=True))
        alpha = jnp.exp2(m_prev - m_new)
        p = jnp.exp2(st - m_new).astype(BF16)
        acc_ref[a, :, cols] = alpha * acc_ref[a, :, cols] + jnp.dot(vt, p, preferred_element_type=F32)
        m_ref[a, :, cols] = m_new

    def finish(r0, n):
        ot = jnp.concatenate([acc_ref[a, 0:HEAD_DIM, 0:n] * (1.0 / acc_ref[a, HEAD_DIM:HEAD_DIM + 1, 0:n])
                              for a in range(2)], axis=0)
        o_ref[pl.ds(r0, n), :] = ot.T.astype(o_ref.dtype)

    def set_queries(qslot, r0, n):
        q = q_ref[pl.ds(r0, n), :]
        for a in range(2):
            qs_ref[qslot, a, 0:n, :] = q * masks[a:a + 1, :]

    def scores(qslot, unit, keys):
        a, c0, nc = unit
        return lax.dot_general(keys, qs_ref[qslot, a, c0:c0 + nc, :], _NT, preferred_element_type=F32)

    def values(a, k0, nk):
        return vt_ref[a * VT_ROWS:(a + 1) * VT_ROWS, pl.ds(k0, nk)]

    def visible(shape, c0, causal_lead):
        row = lax.broadcasted_iota(jnp.int32, shape, 0)
        col = lax.broadcasted_iota(jnp.int32, shape, 1)
        lead = row < FRONT
        lowest = jnp.where(lead, PAD, FRONT)
        highest = jnp.where(lead, col if causal_lead else FRONT, col + (c0 + FRONT))
        return (row >= lowest) & (row <= highest)

    init(FRONT)
    set_queries(0, 0, FRONT)
    lead_units = [(0, 0, FRONT), (1, 0, FRONT)]
    lead_keys = k_ref[0:FRONT, :]
    lead_sts = [scores(0, u, lead_keys) for u in lead_units]
    for u, st in zip(lead_units, lead_sts):
        update(u, st, values(u[0], 0, FRONT), visible(st.shape, 0, True))
    finish(0, FRONT)

    n_sub = tq // QUERY_SUB
    units = [(a, s * QUERY_SUB, QUERY_SUB) for a in range(2) for s in range(n_sub)]
    diag_keys = [FRONT + unit[1] + QUERY_SUB for unit in units]

    def block_start(j):
        return pl.multiple_of(FRONT + j * tq, LANES)

    def park_block(slot, qslot, j):
        keys = k_ref[pl.ds(block_start(j), tq), :]
        for u, unit in enumerate(units):
            s_ref[slot, u] = scores(qslot, unit, keys)

    def consume_block(slot, j):
        for u, unit in enumerate(units):
            update(unit, s_ref[slot, u], values(unit[0], block_start(j), tq), None)

    def park_diag(qslot, i):
        kk = jnp.concatenate([k_ref[0:FRONT, :], k_ref[pl.ds(block_start(i), tq), :]], axis=0)
        for u, (unit, nk) in enumerate(zip(units, diag_keys)):
            sd_refs[u][...] = scores(qslot, unit, kk[0:nk, :])

    def consume_diag(i):
        vts = [jnp.concatenate([values(a, 0, FRONT), values(a, block_start(i), tq)], axis=1) for a in range(2)]
        for u, (unit, nk) in enumerate(zip(units, diag_keys)):
            update(unit, sd_refs[u][...], vts[unit[0]][:, 0:nk], visible((nk, unit[2]), unit[1], False))

    def visible_pairs(qslot, i, n_pairs):
        def pair(t, c):
            park_block(1, qslot, 2 * t + 1)
            consume_block(0, 2 * t)
            park_block(0, qslot, jnp.minimum(2 * t + 2, i - 1))
            consume_block(1, 2 * t + 1)
            return c
        lax.fori_loop(0, n_pairs, pair, 0)

    set_queries(0, block_start(0), tq)

    def two_blocks(g, carry):
        even, odd = 2 * g, 2 * g + 1
        init(tq)
        visible_pairs(0, even, g)
        set_queries(1, block_start(odd), tq)
        park_diag(0, even)
        park_block(0, 1, 0)
        consume_diag(even)
        finish(block_start(even), tq)
        init(tq)
        visible_pairs(1, odd, g)
        park_diag(1, odd)
        consume_block(0, 2 * g)
        nxt = jnp.minimum(even + 2, n_blocks - 1)
        set_queries(0, block_start(nxt), tq)
        park_block(0, 0, 0)
        consume_diag(odd)
        finish(block_start(odd), tq)
        return carry

    lax.fori_loop(0, n_blocks // 2, two_blocks, 0)


def _causal_attn(q, k, vt, masks, tq):
    b, _, lp = vt.shape
    n_pairs = N_HEADS // 2
    n_blocks = (lp - FRONT) // tq
    assert tq % QUERY_SUB == 0 and n_blocks % 2 == 0
    n_sub = tq // QUERY_SUB
    n_units = 2 * n_sub

    def qk_spec():
        return pl.BlockSpec((None, lp, 2 * LANES), lambda bi, p: (bi, 0, p))

    return pl.pallas_call(
        functools.partial(_causal_attn_kernel, tq=tq, n_blocks=n_blocks),
        grid=(b, n_pairs),
        in_specs=[pl.BlockSpec((None, 2, 2 * LANES), lambda bi, p: (p, 0, 0)),
                  qk_spec(), qk_spec(),
                  pl.BlockSpec((None, 2 * VT_ROWS, lp), lambda bi, p: (bi, p, 0))],
        out_specs=pl.BlockSpec((None, lp, LANES), lambda bi, p: (bi, 0, p)),
        out_shape=jax.ShapeDtypeStruct((b, lp, BRANCH_WIDTH), BF16),
        scratch_shapes=[pltpu.VMEM((2, 1, tq), F32), pltpu.VMEM((2, VT_ROWS, tq), F32),
                        pltpu.VMEM((2, 2, tq, 2 * LANES), BF16),
                        pltpu.VMEM((2, n_units, tq, QUERY_SUB), F32),
                        *[pltpu.VMEM((FRONT + (u % n_sub + 1) * QUERY_SUB, QUERY_SUB), F32) for u in range(n_units)]],
        compiler_params=pltpu.CompilerParams(dimension_semantics=("arbitrary", "arbitrary"),
                                             vmem_limit_bytes=VMEM_LIMIT),
        name="causal_attn",
    )(masks, q, k, vt)


def _swa_kernel(sinks_ref, mask_ref, q_ref, qaux_ref, k_ref, kaux_ref, vt_ref, o_ref, bias_ref, s_ref, *, qb, n_blocks):
    masks = mask_ref[...]
    n_band = WINDOW + qb

    def key_bias(shape, n_band_rows, band_shift):
        row = lax.broadcasted_iota(jnp.int32, shape, 0)
        col = lax.broadcasted_iota(jnp.int32, shape, 1)
        band = row < n_band_rows
        lowest = jnp.where(band, col + (band_shift + 1), 0)
        highest = jnp.where(band, col + (band_shift + WINDOW), n_band_rows + N_META)
        return jnp.where((row >= lowest) & (row <= highest), 0.0, NEG)

    def keys(key_slices):
        return jnp.concatenate([jnp.concatenate([k_ref[pl.ds(s, n), :], kaux_ref[pl.ds(s, n), :]], axis=1)
                                for s, n in key_slices], axis=0)

    def values(key_slices):
        return [jnp.concatenate([vt_ref[g * VT_ROWS:(g + 1) * VT_ROWS, pl.ds(s, n)] for s, n in key_slices], axis=1)
                for g in range(SWA_KV_HEADS)]

    def tile_scores(r0, nq, t, kk):
        cols = slice(t * LANES, (t + 1) * LANES)
        qt = jnp.concatenate([q_ref[pl.ds(r0, nq), cols], qaux_ref[pl.ds(r0, nq), cols]], axis=1)
        return [lax.dot_general(kk, qt * masks[a:a + 1, :], _NT, preferred_element_type=F32) for a in range(2)]

    def tile_finish(r0, nq, t, sts, vts, bias):
        outs = []
        for a in range(2):
            sink = sinks_ref[t + 4 * a]
            st = sts[a] + bias
            m = jnp.maximum(jnp.max(st, axis=0, keepdims=True), sink)
            p = jnp.exp(st - m).astype(BF16)
            acc = jnp.dot(vts[a], p, preferred_element_type=F32)
            den = acc[HEAD_DIM:HEAD_DIM + 1, :] + jnp.exp(sink - m)
            outs.append(acc[0:HEAD_DIM, :] * (1.0 / den))
        o_ref[pl.ds(r0, nq), t * LANES:(t + 1) * LANES] = jnp.concatenate(outs, axis=0).T.astype(o_ref.dtype)

    def attend(r0, nq, key_slices, bias):
        kk, vts = keys(key_slices), values(key_slices)
        for t in range(4):
            tile_finish(r0, nq, t, tile_scores(r0, nq, t, kk), vts, bias)

    row = lax.broadcasted_iota(jnp.int32, (FRONT, FRONT), 0)
    col = lax.broadcasted_iota(jnp.int32, (FRONT, FRONT), 1)
    attend(0, FRONT, [(0, FRONT)], jnp.where((row >= PAD) & (row <= col), 0.0, NEG))
    meta = (PAD, N_META)
    attend(FRONT, qb, [(FRONT, qb), meta], key_bias((qb + N_META, qb), qb, -WINDOW))
    bias_ref[...] = key_bias(bias_ref.shape, n_band, 0)

    def block_keys(i):
        return [(pl.multiple_of(FRONT + i * qb - WINDOW, LANES), n_band), meta]

    def park_scores(slot, i, t):
        sts = tile_scores(pl.multiple_of(FRONT + i * qb, LANES), qb, t, keys(block_keys(i)))
        for a in range(2):
            s_ref[slot, a] = sts[a]

    park_scores(0, 1, 0)

    def q_block(i, carry):
        r0 = pl.multiple_of(FRONT + i * qb, LANES)
        vts = values(block_keys(i))
        for t in range(4):
            if t < 3:
                park_scores((t + 1) % 2, i, t + 1)
            else:
                park_scores(0, jnp.minimum(i + 1, n_blocks - 1), 0)
            tile_finish(r0, qb, t, [s_ref[t % 2, a] for a in range(2)], vts, bias_ref[...])
        return carry

    lax.fori_loop(1, n_blocks, q_block, 0)


def _swa_attn(q, k, vt, sinks, consts):
    b, lp, _ = q.shape
    qb = SWA_QUERY_BLOCK
    assert (lp - FRONT) % qb == 0
    full = lambda a: pl.BlockSpec(a.shape, lambda bi: (0,) * a.ndim)
    per_batch = lambda r, c: pl.BlockSpec((None, r, c), lambda bi: (bi, 0, 0))
    return pl.pallas_call(
        functools.partial(_swa_kernel, qb=qb, n_blocks=(lp - FRONT) // qb),
        grid=(b,),
        in_specs=[pl.BlockSpec(memory_space=pltpu.SMEM), full(consts['swa_masks']),
                  per_batch(lp, BRANCH_WIDTH), full(consts['swa_qaux']),
                  per_batch(lp, LANES), full(consts['swa_kaux']),
                  per_batch(SWA_KV_HEADS * VT_ROWS, lp)],
        out_specs=per_batch(lp, BRANCH_WIDTH),
        out_shape=jax.ShapeDtypeStruct((b, lp, BRANCH_WIDTH), BF16),
        scratch_shapes=[pltpu.VMEM((WINDOW + qb + N_META, qb), F32),
                        pltpu.VMEM((2, 2, WINDOW + qb + N_META, qb), F32)],
        compiler_params=pltpu.CompilerParams(dimension_semantics=("arbitrary",), vmem_limit_bytes=VMEM_LIMIT),
        name="swa_attn",
    )(sinks, consts['swa_masks'], q, consts['swa_qaux'], k, consts['swa_kaux'], vt)


def _merge_kernel(h_ref, front_ref, oa_ref, ob_ref, oc_ref, g1_ref, wg_ref, wb_ref, wo_ref, out_ref, *, from_x):
    h = _stream_rows(h_ref, front_ref, pl.program_id(1), 0, out_ref.shape[0], from_x)
    d = h.shape[-1]
    xn = (_rms_rows(h, d) * g1_ref[...]).astype(BF16)
    merged = None
    for n, o_ref in enumerate((oa_ref, ob_ref, oc_ref)):
        gate = jax.nn.sigmoid(jnp.dot(xn, wg_ref[:, n * d:(n + 1) * d], preferred_element_type=F32))
        y = jnp.dot(o_ref[...], wb_ref[n], preferred_element_type=F32)
        merged = gate * y if merged is None else merged + gate * y
    out_ref[...] = h + jnp.dot(merged.astype(BF16), wo_ref[...], preferred_element_type=F32)


def _merge(h, front, oa, ob, oc, p, tm, from_x):
    b, lp, _ = oa.shape
    d = h.shape[-1]
    row_blk = lambda w: pl.BlockSpec((None, tm, w), lambda bi, j: (bi, j, 0))
    whole = [_whole(p[k], pipeline_mode=pl.Buffered(1)) for k in ('g1', 'wg', 'wb', 'wo')]
    ws = [a for a, _ in whole]
    return pl.pallas_call(
        functools.partial(_merge_kernel, from_x=from_x),
        grid=(b, lp // tm),
        in_specs=[_stream_spec(tm, d, from_x), _whole(front)[1], row_blk(BRANCH_WIDTH), row_blk(BRANCH_WIDTH), row_blk(BRANCH_WIDTH)]
        + [spec for _, spec in whole],
        out_specs=row_blk(d),
        out_shape=jax.ShapeDtypeStruct((b, lp, d), F32),
        compiler_params=pltpu.CompilerParams(dimension_semantics=("arbitrary", "arbitrary"),
                                             vmem_limit_bytes=VMEM_LIMIT),
        name="merge",
    )(h, front, oa, ob, oc, *ws)


_FF_CHUNK = 256
FFN_PIECES = 1
_HALO = 8


def _ffn_kernel(h_ref, prev_ref, g2_ref, wup_ref, cw_ref, cb_ref, wdn_ref, out_ref, carry_ref, ubuf_ref, act_ref, *,
                tm, d_ff, real_rows_only):
    j = pl.program_id(1)
    nr = tm // FFN_PIECES

    @pl.when(j == 0)
    def _():
        if real_rows_only:
            prev = prev_ref[...]
            xp = (_rms_rows(prev, prev.shape[-1]) * g2_ref[...]).astype(BF16)
            u_prev = jnp.dot(xp, wup_ref[...], preferred_element_type=F32)
            carry_ref[...] = u_prev[N_META - _HALO:N_META, :]
        else:
            carry_ref[...] = jnp.zeros_like(carry_ref)

    def up(s):
        rows = slice(s * nr, (s + 1) * nr)
        h = h_ref[rows, :]
        xn = _rms_rows(h, h.shape[-1]) * g2_ref[...]
        if not real_rows_only:
            row = j * tm + s * nr + lax.broadcasted_iota(jnp.int32, (nr, 1), 0)
            xn = jnp.where(row >= PAD, xn, 0.0)
        xn = xn.astype(BF16)
        for c in range(d_ff // _FF_CHUNK):
            acts = []
            for half in range(2):
                c0 = half * d_ff + c * _FF_CHUNK
                cols = slice(c0, c0 + _FF_CHUNK)
                u = jnp.dot(xn, wup_ref[:, cols], preferred_element_type=F32)
                ubuf_ref[half, 0:_HALO, :] = carry_ref[:, cols]
                ubuf_ref[half, _HALO:_HALO + nr, :] = u
                carry_ref[:, cols] = u[nr - _HALO:nr, :]
                u1 = ubuf_ref[half, _HALO - 1:_HALO - 1 + nr, :]
                u2 = ubuf_ref[half, _HALO - 2:_HALO - 2 + nr, :]
                acts.append(cb_ref[:, cols] + cw_ref[2:3, cols] * u + cw_ref[1:2, cols] * u1 + cw_ref[0:1, cols] * u2)
            gate, val = acts
            act_ref[rows, c * _FF_CHUNK:(c + 1) * _FF_CHUNK] = (gate * jax.nn.sigmoid(gate) * val).astype(BF16)

    def down(s):
        rows = slice(s * nr, (s + 1) * nr)
        out_ref[rows, :] = h_ref[rows, :] + jnp.dot(act_ref[rows, :], wdn_ref[...], preferred_element_type=F32)

    up(0)
    for s in range(FFN_PIECES):
        if s + 1 < FFN_PIECES:
            up(s + 1)
        down(s)


def _ffn(h, p, tm, real_rows_only=False):
    b, lp, d = h.shape
    row_blk = pl.BlockSpec((None, tm, d), lambda bi, j: (bi, j, 0))
    n_rows = lp - FRONT if real_rows_only else lp
    if real_rows_only:
        h_spec = pl.BlockSpec((None, pl.Element(tm), pl.Element(d)),
                              lambda bi, j: (bi, pl.multiple_of(FRONT + j * tm, LANES), 0))
        prev_spec = pl.BlockSpec((None, pl.Element(N_META), pl.Element(d)), lambda bi, j: (bi, PAD, 0))
    else:
        h_spec = row_blk
        prev_spec = pl.BlockSpec((None, N_META, d), lambda bi, j: (bi, 0, 0))
    whole = [_whole(p[k], pipeline_mode=pl.Buffered(1)) for k in ('g2', 'wup', 'cw', 'cb', 'wdn')]
    ws = [a for a, _ in whole]
    d_ff = ws[-1].shape[-2]
    return pl.pallas_call(
        functools.partial(_ffn_kernel, tm=tm, d_ff=d_ff, real_rows_only=real_rows_only),
        grid=(b, n_rows // tm),
        in_specs=[h_spec, prev_spec] + [spec for _, spec in whole],
        out_specs=row_blk,
        out_shape=jax.ShapeDtypeStruct((b, n_rows, d), F32),
        scratch_shapes=[pltpu.VMEM((_HALO, 2 * d_ff), F32),
                        pltpu.VMEM((2, _HALO + tm, _FF_CHUNK), F32),
                        pltpu.VMEM((tm, d_ff), BF16)],
        compiler_params=pltpu.CompilerParams(dimension_semantics=("arbitrary", "arbitrary"),
                                             vmem_limit_bytes=VMEM_LIMIT),
        name="conv_ffn",
    )(h, h, *ws)


def _constants(lp, tm):
    half = ROPE_DIM // 2
    freqs = ROPE_THETA ** (-np.arange(half, dtype=np.float64) / half)
    lane = np.arange(LANES)
    ang = (np.arange(lp) - PAD).astype(np.float64)[:, None] * freqs[lane % half][None, :]
    sign = np.where((lane % ROPE_DIM) < half, -1.0, 1.0)
    row = np.arange(lp)
    pos_hi, pos_lo = row // LANES, row % LANES
    swa_kaux = np.zeros((lp, LANES), np.float32)
    swa_qaux = np.zeros((lp, BRANCH_WIDTH), np.float32)
    swa_masks = np.zeros((2, 2 * LANES), np.float32)
    for a in range(2):
        swa_masks[a, a * HEAD_DIM:(a + 1) * HEAD_DIM] = 1.0
        swa_masks[a, LANES + 4 * a:LANES + 4 * (a + 1)] = 1.0
        swa_kaux[:, 4 * a:4 * (a + 1)] = np.stack([np.ones(lp), np.ones(lp), pos_hi, pos_lo], axis=1)
        for t in range(4):
            slope = 2.0 ** (-8.0 * (t + 4 * a + 1) / N_HEADS)
            swa_qaux[:, t * LANES + 4 * a:t * LANES + 4 * (a + 1)] = np.stack(
                [-LANES * slope * pos_hi, -slope * pos_lo, np.full(lp, LANES * slope), np.full(lp, slope)], axis=1)
    tri = np.tril(np.ones((tm, tm), np.float32))
    eq = np.zeros((DEC_PARTS, LANES, LANES), np.float32)
    ek = np.zeros((DEC_PARTS, LANES, LANES), np.float32)
    qconst = np.zeros((1, LANES), np.float32)
    kconst = np.zeros((1, LANES), np.float32)
    for hd in range(N_HEADS):
        base = 2 * DEC_PARTS * hd
        for i in range(DEC_PARTS):
            eq[i, hd, base + i] = 1.0
            ek[i, hd, base + DEC_PARTS + i] = -1.0
            qconst[0, base + DEC_PARTS + i] = 1.0
            kconst[0, base + i] = 1.0
    fox_masks = np.zeros((N_HEADS // 2, 2, 2 * LANES), np.float32)
    mla_masks = np.zeros((N_HEADS // 2, 2, 2 * LANES), np.float32)
    for p in range(N_HEADS // 2):
        for a in range(2):
            hd = 2 * p + a
            fox_masks[p, a, a * HEAD_DIM:(a + 1) * HEAD_DIM] = 1.0
            fox_masks[p, a, LANES + 2 * DEC_PARTS * hd:LANES + 2 * DEC_PARTS * (hd + 1)] = 1.0
            mla_masks[p, a, a * HEAD_DIM:(a + 1) * HEAD_DIM] = 1.0
            g = hd % 4
            mla_masks[p, a, LANES + g * ROPE_DIM:LANES + (g + 1) * ROPE_DIM] = 1.0
    return {
        'cos': jnp.asarray(np.cos(ang), F32), 'sin': jnp.asarray(np.sin(ang) * sign[None, :], F32),
        'tri': jnp.asarray(tri, BF16), 'eq': jnp.asarray(eq, BF16), 'ek': jnp.asarray(ek, BF16),
        'qconst': jnp.asarray(qconst), 'kconst': jnp.asarray(kconst),
        'fox_masks': jnp.asarray(fox_masks, BF16), 'mla_masks': jnp.asarray(mla_masks, BF16),
        'swa_masks': jnp.asarray(swa_masks, BF16), 'swa_qaux': jnp.asarray(swa_qaux, BF16),
        'swa_kaux': jnp.asarray(swa_kaux, BF16),
    }


def _swa_head_order(a, axis):
    shape = a.shape
    a = a.reshape(shape[:axis] + (SWA_KV_HEADS, N_HEADS // SWA_KV_HEADS, HEAD_DIM) + shape[axis + 1:])
    return jnp.swapaxes(a, axis, axis + 1).reshape(shape)


_O_FQ, _O_FK, _O_FV, _O_FF = 0, 512, 1024, 1536
_O_CQ = _O_FF + N_HEADS
_O_CKV = _O_CQ + MLA_Q_RANK
_O_KR = _O_CKV + MLA_KV_RANK
_O_SQ = _O_KR + ROPE_DIM
_O_SK = _O_SQ + 512
_O_SV = _O_SK + SWA_KV_HEADS * HEAD_DIM
_O_GATES = _O_SV + SWA_KV_HEADS * HEAD_DIM


def _stacked_weights(w_in, w_branch, w_o, ffn_w_up, ffn_w_down):
    n_layers, d, _ = w_in.shape
    col = lambda c0, n: w_in[:, :, c0:c0 + n]
    wa = jnp.concatenate([
        col(_O_FQ, 512), col(_O_FK, 512), _swa_head_order(col(_O_SQ, 512), 2), col(_O_SK, 128),
        col(_O_CQ, MLA_Q_RANK), col(_O_CKV, MLA_KV_RANK),
        jnp.tile(col(_O_KR, ROPE_DIM), (1, 1, LANES // ROPE_DIM)),
        col(_O_FF, N_HEADS), jnp.zeros((n_layers, d, LANES - N_HEADS), w_in.dtype)], axis=2).astype(BF16)
    wb = jnp.concatenate([w_branch[:, :2], _swa_head_order(w_branch[:, 2:3], 2)], axis=1).astype(BF16)
    return {'wa': wa, 'wg': w_in[:, :, _O_GATES:].astype(BF16), 'wb': wb, 'wo': w_o.astype(BF16),
            'wup': ffn_w_up.astype(BF16), 'wdn': ffn_w_down.astype(BF16)}


def _layer_params(l, stacked, norm1_g, w_in, fox_forget_b, fox_q_g, fox_k_g, mla_q_a_g, mla_w_q_up, mla_kv_a_g,
                  mla_w_kv_up, mla_q_g, mla_k_g, swa_q_g, swa_k_g, swa_sinks, w_branch, w_o,
                  norm2_g, ffn_w_up, ffn_conv_w, ffn_conv_b, ffn_w_down):
    w = w_in[l]
    tile_row = lambda g, n: jnp.tile(g, n)[None, :].astype(F32)
    width = HEAD_DIM + ROPE_DIM
    qcols = np.concatenate([np.concatenate([np.arange(hd * width, hd * width + HEAD_DIM) for hd in range(N_HEADS)]),
                            np.concatenate([np.arange(hd * width + HEAD_DIM, (hd + 1) * width) for hd in range(N_HEADS)])])
    kcols = np.concatenate([np.arange(hd * 2 * HEAD_DIM, hd * 2 * HEAD_DIM + HEAD_DIM) for hd in range(N_HEADS)])
    vcols = kcols + HEAD_DIM
    fb = jnp.concatenate([fox_forget_b[l], jnp.zeros((LANES - N_HEADS,), F32)])[None, :]
    return {
        'g1': norm1_g[l][None, :], 'wa': (stacked['wa'], l),
        'gfq': tile_row(fox_q_g[l], N_HEADS) * (LOG2E * HEAD_DIM ** -0.5), 'gfk': tile_row(fox_k_g[l], N_HEADS),
        'gsq': tile_row(swa_q_g[l], N_HEADS) * (HEAD_DIM ** -0.5), 'gsk': tile_row(swa_k_g[l], SWA_KV_HEADS),
        'fb': fb,
        'gcq': mla_q_a_g[l][None, :], 'wqu': mla_w_q_up[l][:, qcols].astype(BF16),
        'gckv': mla_kv_a_g[l][None, :], 'wkvu': mla_w_kv_up[l][:, kcols].astype(BF16),
        'wvtf': w[:, _O_FV:_O_FV + 512].T.astype(BF16), 'wvtm': mla_w_kv_up[l][:, vcols].T.astype(BF16),
        'wvts': w[:, _O_SV:_O_SV + SWA_KV_HEADS * HEAD_DIM].T.astype(BF16),
        'gmq_n': tile_row(mla_q_g[l][:HEAD_DIM], N_HEADS) * (LOG2E * width ** -0.5),
        'gmq_r': tile_row(mla_q_g[l][HEAD_DIM:], LANES // ROPE_DIM) * (LOG2E * width ** -0.5),
        'gmk_n': tile_row(mla_k_g[l][:HEAD_DIM], N_HEADS),
        'gmk_r': tile_row(mla_k_g[l][HEAD_DIM:], LANES // ROPE_DIM),
        'sinks': swa_sinks[l].astype(F32),
        'wg': (stacked['wg'], l), 'wb': (stacked['wb'], l), 'wo': (stacked['wo'], l),
        'g2': norm2_g[l][None, :], 'wup': (stacked['wup'], l), 'cw': ffn_conv_w[l], 'cb': ffn_conv_b[l][None, :],
        'wdn': (stacked['wdn'], l),
    }


def kernel(x, meta_tokens, norm1_g, w_in, fox_forget_b, fox_q_g, fox_k_g, mla_q_a_g, mla_w_q_up, mla_kv_a_g, mla_w_kv_up, mla_q_g, mla_k_g, swa_q_g, swa_k_g, swa_sinks, w_branch, w_o, norm2_g, ffn_w_up, ffn_conv_w, ffn_conv_b, ffn_w_down):
    b, seq, d = x.shape
    assert seq % LANES == 0 and meta_tokens.shape == (N_META, d)
    lp = FRONT + seq
    tm = _row_tile(lp)
    tq = CAUSAL_BLOCK if seq % (2 * CAUSAL_BLOCK) == 0 else _row_tile(seq)
    consts = _constants(lp, IN_SUB)
    front = jnp.concatenate([jnp.zeros((PAD, d), x.dtype), meta_tokens.astype(x.dtype)], axis=0)
    h = x
    weights = (norm1_g, w_in, fox_forget_b, fox_q_g, fox_k_g, mla_q_a_g, mla_w_q_up, mla_kv_a_g, mla_w_kv_up,
               mla_q_g, mla_k_g, swa_q_g, swa_k_g, swa_sinks, w_branch, w_o, norm2_g, ffn_w_up, ffn_conv_w,
               ffn_conv_b, ffn_w_down)
    stacked = _stacked_weights(w_in, w_branch, w_o, ffn_w_up, ffn_w_down)
    for l in range(w_in.shape[0]):
        p = _layer_params(l, stacked, *weights)
        qf, kf, vtf, qm, km, vtm, sq, sk, vts = _in_proj(h, front, p, consts, tm, from_x=l == 0)
        out_a = _causal_attn(qf, kf, vtf, consts['fox_masks'], tq)
        out_b = _causal_attn(qm, km, vtm, consts['mla_masks'], tq)
        out_c = _swa_attn(sq, sk, vts, p['sinks'], consts)
        h = _merge(h, front, out_a, out_b, out_c, p, tm, from_x=l == 0)
        last = l == w_in.shape[0] - 1
        h = _ffn(h, p, _row_tile(seq) if last else tm, real_rows_only=last)
    return h
```

```python
import functools
import math

import numpy as np
import jax
import jax.numpy as jnp
from jax import lax
from jax.experimental import pallas as pl
from jax.experimental.pallas import tpu as pltpu

F32 = jnp.float32
BF16 = jnp.bfloat16

N_META = 16
EPS = 1e-6
N_HEADS = 8
HEAD_DIM = 64
ROPE_DIM = 32
MLA_Q_RANK = 256
MLA_KV_RANK = 128
SWA_KV_HEADS = 2
WINDOW = 128
ROPE_THETA = 10000.0
BRANCH_WIDTH = N_HEADS * HEAD_DIM
N_BRANCH = 3
CONV_WIDTH = 3

LANES = 128
FRONT = 128
PAD = FRONT - N_META
NEG = -1e30
DEC_PARTS = 3
LOG2E = math.log2(math.e)
ONES_ROWS = 16
VT_ROWS = HEAD_DIM + ONES_ROWS
QUERY_SUB = 256
CAUSAL_BLOCK = 1024
IN_SUB = 128
SWA_QUERY_BLOCK = 256
VMEM_LIMIT = 56 * 1024 * 1024

_A_FQ, _A_FK, _A_SQ, _A_SK, _A_CQ, _A_CKV, _A_KR, _A_FF = 0, 512, 1024, 1536, 1664, 1920, 2048, 2176
_NT = (((1,), (1,)), ((), ()))


def _row_tile(lp):
    best = 128
    for t in range(128, 513, 128):
        if lp % t == 0:
            best = t
    return best


def _whole(w, **kw):
    if isinstance(w, tuple):
        arr, layer = w
        return arr, pl.BlockSpec((None,) + arr.shape[1:], lambda *g: (layer,) + (0,) * (arr.ndim - 1), **kw)
    return w, pl.BlockSpec(w.shape, lambda *g: (0,) * w.ndim, **kw)


def _stream_spec(tm, d, from_x):
    if not from_x:
        return pl.BlockSpec((None, tm, d), lambda bi, j: (bi, j, 0))
    return pl.BlockSpec((None, pl.Element(tm), pl.Element(d)),
                        lambda bi, j: (bi, pl.multiple_of(jnp.maximum(j * tm - FRONT, 0), LANES), 0))


def _stream_rows(h_ref, front_ref, j, r0, n, from_x):
    if not from_x:
        return h_ref[r0:r0 + n, :]
    lead = [front_ref[r0:min(r0 + n, FRONT), :]] if r0 < FRONT else []
    rest = [h_ref[max(r0 - FRONT, 0):r0 + n - FRONT, :]] if r0 + n > FRONT else []
    first_tile = jnp.concatenate(lead + rest, axis=0)
    return jnp.where(j == 0, first_tile, h_ref[r0:r0 + n, :])


def _lane_iota(shape):
    return lax.broadcasted_iota(jnp.int32, shape, len(shape) - 1)


def _rms_rows(y, width):
    ss = jnp.sum(y * y, axis=-1, keepdims=True)
    return y * lax.rsqrt(ss * (1.0 / width) + EPS)


def _half_sums(y2):
    lane = _lane_iota(y2.shape)
    lo = jnp.sum(jnp.where(lane < HEAD_DIM, y2, 0.0), axis=-1, keepdims=True)
    hi = jnp.sum(jnp.where(lane >= HEAD_DIM, y2, 0.0), axis=-1, keepdims=True)
    return lo, hi


def _headnorm64_tile(yt):
    lo, hi = _half_sums(yt * yt)
    lane = _lane_iota(yt.shape)
    r = jnp.where(lane < HEAD_DIM, lax.rsqrt(lo * (1.0 / HEAD_DIM) + EPS), lax.rsqrt(hi * (1.0 / HEAD_DIM) + EPS))
    return yt * r


def _rotate_half(x):
    lane = _lane_iota(x.shape)
    half = ROPE_DIM // 2
    return jnp.where((lane % ROPE_DIM) < half, pltpu.roll(x, LANES - half, 1), pltpu.roll(x, half, 1))


def _quad_select(lane, vals):
    g = lane // ROPE_DIM
    out = jnp.where(g == 0, vals[0], vals[1])
    out = jnp.where(g == 2, vals[2], out)
    return jnp.where(g == 3, vals[3], out)


def _in_proj_kernel(h_ref, front_ref, g1_ref, wa_ref, gfq_ref, gfk_ref, gsq_ref, gsk_ref, fb_ref,
                    gcq_ref, wqu_ref, gckv_ref, wkvu_ref, gmq_n_ref, gmq_r_ref, gmk_n_ref, gmk_r_ref,
                    wvtf_ref, wvtm_ref, wvts_ref,
                    cos_ref, sin_ref, tri_ref, eq_ref, ek_ref, qconst_ref, kconst_ref,
                    qf_ref, kf_ref, vtf_ref, qm_ref, km_ref, vtm_ref, sq_ref, sk_ref, vts_ref,
                    carry_ref, xn_ref, y_ref, *, tm, from_x):
    j = pl.program_id(1)

    def store_vt(vt_ref, wt_ref, x, cols):
        yt = lax.dot_general(wt_ref[...], x, _NT, preferred_element_type=F32)
        ones = jnp.ones((ONES_ROWS, yt.shape[1]), BF16)
        for hd in range(yt.shape[0] // HEAD_DIM):
            vt_ref[hd * VT_ROWS:hd * VT_ROWS + HEAD_DIM, cols] = yt[hd * HEAD_DIM:(hd + 1) * HEAD_DIM, :].astype(BF16)
            vt_ref[hd * VT_ROWS + HEAD_DIM:(hd + 1) * VT_ROWS, cols] = ones

    @pl.when(j == 0)
    def _():
        carry_ref[...] = jnp.zeros_like(carry_ref)

    n = IN_SUB
    lane = _lane_iota((n, LANES))
    quad_id = lane // ROPE_DIM
    width = HEAD_DIM + ROPE_DIM

    def project(s):
        h = _stream_rows(h_ref, front_ref, j, s * n, n, from_x)
        xn = (_rms_rows(h, h.shape[-1]) * g1_ref[...]).astype(BF16)
        xn_ref[s * n:(s + 1) * n, :] = xn
        y_ref[s % 2] = jnp.dot(xn, wa_ref[...], preferred_element_type=F32)

    def finish(s):
        rows = slice(s * n, (s + 1) * n)

        def proj(c0, w):
            return y_ref[s % 2, :, c0:c0 + w]

        yq = proj(_A_FQ, 512)
        yk = proj(_A_FK, 512)
        for t in range(4):
            sl = slice(t * LANES, (t + 1) * LANES)
            qf_ref[rows, 2 * t * LANES:(2 * t + 1) * LANES] = (_headnorm64_tile(yq[:, sl]) * gfq_ref[:, sl]).astype(BF16)
            kf_ref[rows, 2 * t * LANES:(2 * t + 1) * LANES] = (_headnorm64_tile(yk[:, sl]) * gfk_ref[:, sl]).astype(BF16)

        z = proj(_A_FF, LANES) + fb_ref[...]
        ls = jnp.minimum(z, 0.0) - jnp.log1p(jnp.exp(-jnp.abs(z)))
        row = j * tm + s * n + lax.broadcasted_iota(jnp.int32, ls.shape, 0)
        ls = jnp.where((lane < N_HEADS) & (row >= PAD), ls * LOG2E, 0.0)
        tri = tri_ref[...]
        c = carry_ref[0:1, :]
        rem = ls
        for _ in range(DEC_PARTS):
            part = rem.astype(BF16)
            c = c + jnp.dot(tri, part, preferred_element_type=F32)
            rem = rem - part.astype(F32)
        carry_ref[0:1, :] = c[n - 1:n, :]
        qd = qconst_ref[...]
        kd = kconst_ref[...]
        rem = c
        for i in range(DEC_PARTS):
            part = rem.astype(BF16)
            qd = qd + jnp.dot(part, eq_ref[i], preferred_element_type=F32)
            kd = kd + jnp.dot(part, ek_ref[i], preferred_element_type=F32)
            rem = rem - part.astype(F32)
        qd = qd.astype(BF16)
        kd = kd.astype(BF16)
        for t in range(4):
            qf_ref[rows, (2 * t + 1) * LANES:(2 * t + 2) * LANES] = qd
            kf_ref[rows, (2 * t + 1) * LANES:(2 * t + 2) * LANES] = kd

        ysq = proj(_A_SQ, 512)
        for t in range(4):
            sl = slice(t * LANES, (t + 1) * LANES)
            sq_ref[rows, sl] = (_headnorm64_tile(ysq[:, sl]) * gsq_ref[:, sl]).astype(BF16)
        sk_ref[rows, :] = (_headnorm64_tile(proj(_A_SK, LANES)) * gsk_ref[...]).astype(BF16)

        cos = cos_ref[rows, :]
        sin = sin_ref[rows, :]

        def rope(x):
            return x * cos + _rotate_half(x) * sin

        cq = (_rms_rows(proj(_A_CQ, MLA_Q_RANK), MLA_Q_RANK) * gcq_ref[...]).astype(BF16)
        yq = jnp.dot(cq, wqu_ref[...], preferred_element_type=F32)
        ss = []
        for t in range(4):
            lo, hi = _half_sums(jnp.square(yq[:, t * LANES:(t + 1) * LANES]))
            ss += [lo, hi]
        for u in range(2):
            y2 = jnp.square(yq[:, 512 + u * LANES:512 + (u + 1) * LANES])
            for g in range(4):
                ss[4 * u + g] = ss[4 * u + g] + jnp.sum(jnp.where(quad_id == g, y2, 0.0), axis=-1, keepdims=True)
        rq = [lax.rsqrt(x * (1.0 / width) + EPS) for x in ss]
        q_rope = []
        for u in range(2):
            sl = slice(512 + u * LANES, 512 + (u + 1) * LANES)
            x = yq[:, sl] * _quad_select(lane, rq[4 * u:4 * u + 4]) * gmq_r_ref[...]
            q_rope.append(rope(x).astype(BF16))
        for t in range(4):
            sl = slice(t * LANES, (t + 1) * LANES)
            r = jnp.where(lane < HEAD_DIM, rq[2 * t], rq[2 * t + 1])
            qm_ref[rows, 2 * t * LANES:(2 * t + 1) * LANES] = (yq[:, sl] * r * gmq_n_ref[:, sl]).astype(BF16)
            qm_ref[rows, (2 * t + 1) * LANES:(2 * t + 2) * LANES] = q_rope[t // 2]

        ckv = (_rms_rows(proj(_A_CKV, MLA_KV_RANK), MLA_KV_RANK) * gckv_ref[...]).astype(BF16)
        ykv = jnp.dot(ckv, wkvu_ref[...], preferred_element_type=F32)
        kr4 = proj(_A_KR, LANES)
        ss_rope = jnp.sum(kr4 * kr4, axis=-1, keepdims=True) * 0.25
        rk = []
        for t in range(4):
            lo, hi = _half_sums(jnp.square(ykv[:, t * LANES:(t + 1) * LANES]))
            rk += [lax.rsqrt((lo + ss_rope) * (1.0 / width) + EPS), lax.rsqrt((hi + ss_rope) * (1.0 / width) + EPS)]
        kr_base = rope(kr4 * gmk_r_ref[...])
        k_rope = [(kr_base * _quad_select(lane, rk[4 * u:4 * u + 4])).astype(BF16) for u in range(2)]
        for t in range(4):
            sl = slice(t * LANES, (t + 1) * LANES)
            r = jnp.where(lane < HEAD_DIM, rk[2 * t], rk[2 * t + 1])
            km_ref[rows, 2 * t * LANES:(2 * t + 1) * LANES] = (ykv[:, sl] * r * gmk_n_ref[:, sl]).astype(BF16)
            km_ref[rows, (2 * t + 1) * LANES:(2 * t + 2) * LANES] = k_rope[t // 2]
        store_vt(vtm_ref, wvtm_ref, ckv, rows)

    project(0)
    for s in range(tm // n):
        if s + 1 < tm // n:
            project(s + 1)
        else:
            store_vt(vtf_ref, wvtf_ref, xn_ref[...], slice(0, tm))
            store_vt(vts_ref, wvts_ref, xn_ref[...], slice(0, tm))
        finish(s)


def _in_proj(h, front, p, consts, tm, from_x):
    b, _, d = h.shape
    lp = consts['cos'].shape[0]
    nt = lp // tm
    row_blk = lambda w: pl.BlockSpec((None, tm, w), lambda bi, j: (bi, j, 0))
    full = lambda a: pl.BlockSpec(a.shape, lambda bi, j: (0,) * a.ndim)
    tab = pl.BlockSpec((tm, LANES), lambda bi, j: (j, 0))
    ins = [h, front, p['g1'], p['wa'], p['gfq'], p['gfk'], p['gsq'], p['gsk'], p['fb'],
           p['gcq'], p['wqu'], p['gckv'], p['wkvu'], p['gmq_n'], p['gmq_r'], p['gmk_n'], p['gmk_r'],
           p['wvtf'], p['wvtm'], p['wvts']]
    whole = [_whole(a) for a in ins[1:]]
    ins = [h] + [a for a, _ in whole]
    in_specs = [_stream_spec(tm, d, from_x)] + [spec for _, spec in whole]
    ins += [consts['cos'], consts['sin'], consts['tri'], consts['eq'], consts['ek'], consts['qconst'], consts['kconst']]
    in_specs += [tab, tab] + [full(consts[k]) for k in ('tri', 'eq', 'ek', 'qconst', 'kconst')]
    outs = [('row', 1024), ('row', 1024), ('vt', N_HEADS), ('row', 1024), ('row', 1024), ('vt', N_HEADS),
            ('row', 512), ('row', 128), ('vt', SWA_KV_HEADS)]
    out_shape = [jax.ShapeDtypeStruct((b, lp, w) if kind == 'row' else (b, w * VT_ROWS, lp), BF16) for kind, w in outs]
    out_specs = [row_blk(w) if kind == 'row' else pl.BlockSpec((None, w * VT_ROWS, tm), lambda bi, j: (bi, 0, j))
                 for kind, w in outs]
    return pl.pallas_call(
        functools.partial(_in_proj_kernel, tm=tm, from_x=from_x),
        grid=(b, nt),
        in_specs=in_specs,
        out_specs=out_specs,
        out_shape=out_shape,
        scratch_shapes=[pltpu.VMEM((8, LANES), F32), pltpu.VMEM((tm, d), BF16),
                        pltpu.VMEM((2, IN_SUB, p['wa'][0].shape[-1]), F32)],
        compiler_params=pltpu.CompilerParams(dimension_semantics=("arbitrary", "arbitrary"),
                                             vmem_limit_bytes=VMEM_LIMIT),
        name="in_proj",
    )(*ins)


def _causal_attn_kernel(mask_ref, q_ref, k_ref, vt_ref, o_ref, m_ref, acc_ref, qs_ref, s_ref, *sd_refs, tq, n_blocks):
    masks = mask_ref[...]

    def init(n):
        m_ref[:, :, 0:n] = jnp.full((2, 1, n), NEG, F32)
        acc_ref[:, :, 0:n] = jnp.zeros((2, VT_ROWS, n), F32)

    def update(unit, st, vt, visible):
        a, c0, nc = unit
        if visible is not None:
            st = jnp.where(visible, st, NEG)
        cols = slice(c0, c0 + nc)
        m_prev = m_ref[a, :, cols]
        m_new = jnp.maximum(m_prev, jnp.max(st, axis=0, keepdims=True))
        alpha = jnp.exp2(m_prev - m_new)
        p = jnp.exp2(st - m_new).astype(BF16)
        acc_ref[a, :, cols] = alpha * acc_ref[a, :, cols] + jnp.dot(vt, p, preferred_element_type=F32)
        m_ref[a, :, cols] = m_new

    def finish(r0, n):
        ot = jnp.concatenate([acc_ref[a, 0:HEAD_DIM, 0:n] * (1.0 / acc_ref[a, HEAD_DIM:HEAD_DIM + 1, 0:n])
                              for a in range(2)], axis=0)
        o_ref[pl.ds(r0, n), :] = ot.T.astype(o_ref.dtype)

    def set_queries(qslot, r0, n):
        q = q_ref[pl.ds(r0, n), :]
        for a in range(2):
            qs_ref[qslot, a, 0:n, :] = q * masks[a:a + 1, :]

    def scores(qslot, unit, keys):
        a, c0, nc = unit
        return lax.dot_general(keys, qs_ref[qslot, a, c0:c0 + nc, :], _NT, preferred_element_type=F32)

    def values(a, k0, nk):
        return vt_ref[a * VT_ROWS:(a + 1) * VT_ROWS, pl.ds(k0, nk)]

    def visible(shape, c0, causal_lead):
        row = lax.broadcasted_iota(jnp.int32, shape, 0)
        col = lax.broadcasted_iota(jnp.int32, shape, 1)
        lead = row < FRONT
        lowest = jnp.where(lead, PAD, FRONT)
        highest = jnp.where(lead, col if causal_lead else FRONT, col + (c0 + FRONT))
        return (row >= lowest) & (row <= highest)

    init(FRONT)
    set_queries(0, 0, FRONT)
    lead_units = [(0, 0, FRONT), (1, 0, FRONT)]
    lead_keys = k_ref[0:FRONT, :]
    lead_sts = [scores(0, u, lead_keys) for u in lead_units]
    for u, st in zip(lead_units, lead_sts):
        update(u, st, values(u[0], 0, FRONT), visible(st.shape, 0, True))
    finish(0, FRONT)

    n_sub = tq // QUERY_SUB
    units = [(a, s * QUERY_SUB, QUERY_SUB) for a in range(2) for s in range(n_sub)]
    diag_keys = [FRONT + unit[1] + QUERY_SUB for unit in units]

    def block_start(j):
        return pl.multiple_of(FRONT + j * tq, LANES)

    def park_block(slot, qslot, j):
        keys = k_ref[pl.ds(block_start(j), tq), :]
        for u, unit in enumerate(units):
            s_ref[slot, u] = scores(qslot, unit, keys)

    def consume_block(slot, j):
        for u, unit in enumerate(units):
            update(unit, s_ref[slot, u], values(unit[0], block_start(j), tq), None)

    def park_diag(qslot, i):
        kk = jnp.concatenate([k_ref[0:FRONT, :], k_ref[pl.ds(block_start(i), tq), :]], axis=0)
        for u, (unit, nk) in enumerate(zip(units, diag_keys)):
            sd_refs[u][...] = scores(qslot, unit, kk[0:nk, :])

    def consume_diag(i):
        vts = [jnp.concatenate([values(a, 0, FRONT), values(a, block_start(i), tq)], axis=1) for a in range(2)]
        for u, (unit, nk) in enumerate(zip(units, diag_keys)):
            update(unit, sd_refs[u][...], vts[unit[0]][:, 0:nk], visible((nk, unit[2]), unit[1], False))

    def visible_pairs(qslot, i, n_pairs):
        def pair(t, c):
            park_block(1, qslot, 2 * t + 1)
            consume_block(0, 2 * t)
            park_block(0, qslot, jnp.minimum(2 * t + 2, i - 1))
            consume_block(1, 2 * t + 1)
            return c
        lax.fori_loop(0, n_pairs, pair, 0)

    set_queries(0, block_start(0), tq)

    def two_blocks(g, carry):
        even, odd = 2 * g, 2 * g + 1
        init(tq)
        visible_pairs(0, even, g)
        set_queries(1, block_start(odd), tq)
        park_diag(0, even)
        park_block(0, 1, 0)
        consume_diag(even)
        finish(block_start(even), tq)
        init(tq)
        visible_pairs(1, odd, g)
        park_diag(1, odd)
        consume_block(0, 2 * g)
        nxt = jnp.minimum(even + 2, n_blocks - 1)
        set_queries(0, block_start(nxt), tq)
        park_block(0, 0, 0)
        consume_diag(odd)
        finish(block_start(odd), tq)
        return carry

    lax.fori_loop(0, n_blocks // 2, two_blocks, 0)


def _causal_attn(q, k, vt, masks, tq):
    b, _, lp = vt.shape
    n_pairs = N_HEADS // 2
    n_blocks = (lp - FRONT) // tq
    assert tq % QUERY_SUB == 0 and n_blocks % 2 == 0
    n_sub = tq // QUERY_SUB
    n_units = 2 * n_sub

    def qk_spec():
        return pl.BlockSpec((None, lp, 2 * LANES), lambda bi, p: (bi, 0, p))

    return pl.pallas_call(
        functools.partial(_causal_attn_kernel, tq=tq, n_blocks=n_blocks),
        grid=(b, n_pairs),
        in_specs=[pl.BlockSpec((None, 2, 2 * LANES), lambda bi, p: (p, 0, 0)),
                  qk_spec(), qk_spec(),
                  pl.BlockSpec((None, 2 * VT_ROWS, lp), lambda bi, p: (bi, p, 0))],
        out_specs=pl.BlockSpec((None, lp, LANES), lambda bi, p: (bi, 0, p)),
        out_shape=jax.ShapeDtypeStruct((b, lp, BRANCH_WIDTH), BF16),
        scratch_shapes=[pltpu.VMEM((2, 1, tq), F32), pltpu.VMEM((2, VT_ROWS, tq), F32),
                        pltpu.VMEM((2, 2, tq, 2 * LANES), BF16),
                        pltpu.VMEM((2, n_units, tq, QUERY_SUB), F32),
                        *[pltpu.VMEM((FRONT + (u % n_sub + 1) * QUERY_SUB, QUERY_SUB), F32) for u in range(n_units)]],
        compiler_params=pltpu.CompilerParams(dimension_semantics=("arbitrary", "arbitrary"),
                                             vmem_limit_bytes=VMEM_LIMIT),
        name="causal_attn",
    )(masks, q, k, vt)


def _swa_kernel(sinks_ref, mask_ref, q_ref, qaux_ref, k_ref, kaux_ref, vt_ref, o_ref, bias_ref, s_ref, *, qb, n_blocks):
    masks = mask_ref[...]
    n_band = WINDOW + qb

    def key_bias(shape, n_band_rows, band_shift):
        row = lax.broadcasted_iota(jnp.int32, shape, 0)
        col = lax.broadcasted_iota(jnp.int32, shape, 1)
        band = row < n_band_rows
        lowest = jnp.where(band, col + (band_shift + 1), 0)
        highest = jnp.where(band, col + (band_shift + WINDOW), n_band_rows + N_META)
        return jnp.where((row >= lowest) & (row <= highest), 0.0, NEG)

    def keys(key_slices):
        return jnp.concatenate([jnp.concatenate([k_ref[pl.ds(s, n), :], kaux_ref[pl.ds(s, n), :]], axis=1)
                                for s, n in key_slices], axis=0)

    def values(key_slices):
        return [jnp.concatenate([vt_ref[g * VT_ROWS:(g + 1) * VT_ROWS, pl.ds(s, n)] for s, n in key_slices], axis=1)
                for g in range(SWA_KV_HEADS)]

    def tile_scores(r0, nq, t, kk):
        cols = slice(t * LANES, (t + 1) * LANES)
        qt = jnp.concatenate([q_ref[pl.ds(r0, nq), cols], qaux_ref[pl.ds(r0, nq), cols]], axis=1)
        return [lax.dot_general(kk, qt * masks[a:a + 1, :], _NT, preferred_element_type=F32) for a in range(2)]

    def tile_finish(r0, nq, t, sts, vts, bias):
        outs = []
        for a in range(2):
            sink = sinks_ref[t + 4 * a]
            st = sts[a] + bias
            m = jnp.maximum(jnp.max(st, axis=0, keepdims=True), sink)
            p = jnp.exp(st - m).astype(BF16)
            acc = jnp.dot(vts[a], p, preferred_element_type=F32)
            den = acc[HEAD_DIM:HEAD_DIM + 1, :] + jnp.exp(sink - m)
            outs.append(acc[0:HEAD_DIM, :] * (1.0 / den))
        o_ref[pl.ds(r0, nq), t * LANES:(t + 1) * LANES] = jnp.concatenate(outs, axis=0).T.astype(o_ref.dtype)

    def attend(r0, nq, key_slices, bias):
        kk, vts = keys(key_slices), values(key_slices)
        for t in range(4):
            tile_finish(r0, nq, t, tile_scores(r0, nq, t, kk), vts, bias)

    row = lax.broadcasted_iota(jnp.int32, (FRONT, FRONT), 0)
    col = lax.broadcasted_iota(jnp.int32, (FRONT, FRONT), 1)
    attend(0, FRONT, [(0, FRONT)], jnp.where((row >= PAD) & (row <= col), 0.0, NEG))
    meta = (PAD, N_META)
    attend(FRONT, qb, [(FRONT, qb), meta], key_bias((qb + N_META, qb), qb, -WINDOW))
    bias_ref[...] = key_bias(bias_ref.shape, n_band, 0)

    def block_keys(i):
        return [(pl.multiple_of(FRONT + i * qb - WINDOW, LANES), n_band), meta]

    def park_scores(slot, i, t):
        sts = tile_scores(pl.multiple_of(FRONT + i * qb, LANES), qb, t, keys(block_keys(i)))
        for a in range(2):
            s_ref[slot, a] = sts[a]

    park_scores(0, 1, 0)

    def q_block(i, carry):
        r0 = pl.multiple_of(FRONT + i * qb, LANES)
        vts = values(block_keys(i))
        for t in range(4):
            if t < 3:
                park_scores((t + 1) % 2, i, t + 1)
            else:
                park_scores(0, jnp.minimum(i + 1, n_blocks - 1), 0)
            tile_finish(r0, qb, t, [s_ref[t % 2, a] for a in range(2)], vts, bias_ref[...])
        return carry

    lax.fori_loop(1, n_blocks, q_block, 0)


def _swa_attn(q, k, vt, sinks, consts):
    b, lp, _ = q.shape
    qb = SWA_QUERY_BLOCK
    assert (lp - FRONT) % qb == 0
    full = lambda a: pl.BlockSpec(a.shape, lambda bi: (0,) * a.ndim)
    per_batch = lambda r, c: pl.BlockSpec((None, r, c), lambda bi: (bi, 0, 0))
    return pl.pallas_call(
        functools.partial(_swa_kernel, qb=qb, n_blocks=(lp - FRONT) // qb),
        grid=(b,),
        in_specs=[pl.BlockSpec(memory_space=pltpu.SMEM), full(consts['swa_masks']),
                  per_batch(lp, BRANCH_WIDTH), full(consts['swa_qaux']),
                  per_batch(lp, LANES), full(consts['swa_kaux']),
                  per_batch(SWA_KV_HEADS * VT_ROWS, lp)],
        out_specs=per_batch(lp, BRANCH_WIDTH),
        out_shape=jax.ShapeDtypeStruct((b, lp, BRANCH_WIDTH), BF16),
        scratch_shapes=[pltpu.VMEM((WINDOW + qb + N_META, qb), F32),
                        pltpu.VMEM((2, 2, WINDOW + qb + N_META, qb), F32)],
        compiler_params=pltpu.CompilerParams(dimension_semantics=("arbitrary",), vmem_limit_bytes=VMEM_LIMIT),
        name="swa_attn",
    )(sinks, consts['swa_masks'], q, consts['swa_qaux'], k, consts['swa_kaux'], vt)


def _merge_kernel(h_ref, front_ref, oa_ref, ob_ref, oc_ref, g1_ref, wg_ref, wb_ref, wo_ref, out_ref, *, from_x):
    h = _stream_rows(h_ref, front_ref, pl.program_id(1), 0, out_ref.shape[0], from_x)
    d = h.shape[-1]
    xn = (_rms_rows(h, d) * g1_ref[...]).astype(BF16)
    merged = None
    for n, o_ref in enumerate((oa_ref, ob_ref, oc_ref)):
        gate = jax.nn.sigmoid(jnp.dot(xn, wg_ref[:, n * d:(n + 1) * d], preferred_element_type=F32))
        y = jnp.dot(o_ref[...], wb_ref[n], preferred_element_type=F32)
        merged = gate * y if merged is None else merged + gate * y
    out_ref[...] = h + jnp.dot(merged.astype(BF16), wo_ref[...], preferred_element_type=F32)


def _merge(h, front, oa, ob, oc, p, tm, from_x):
    b, lp, _ = oa.shape
    d = h.shape[-1]
    row_blk = lambda w: pl.BlockSpec((None, tm, w), lambda bi, j: (bi, j, 0))
    whole = [_whole(p[k], pipeline_mode=pl.Buffered(1)) for k in ('g1', 'wg', 'wb', 'wo')]
    ws = [a for a, _ in whole]
    return pl.pallas_call(
        functools.partial(_merge_kernel, from_x=from_x),
        grid=(b, lp // tm),
        in_specs=[_stream_spec(tm, d, from_x), _whole(front)[1], row_blk(BRANCH_WIDTH), row_blk(BRANCH_WIDTH), row_blk(BRANCH_WIDTH)]
        + [spec for _, spec in whole],
        out_specs=row_blk(d),
        out_shape=jax.ShapeDtypeStruct((b, lp, d), F32),
        compiler_params=pltpu.CompilerParams(dimension_semantics=("arbitrary", "arbitrary"),
                                             vmem_limit_bytes=VMEM_LIMIT),
        name="merge",
    )(h, front, oa, ob, oc, *ws)


_FF_CHUNK = 256
FFN_PIECES = 1
_HALO = 8


def _ffn_kernel(h_ref, prev_ref, g2_ref, wup_ref, cw_ref, cb_ref, wdn_ref, out_ref, carry_ref, ubuf_ref, act_ref, *,
                tm, d_ff, real_rows_only):
    j = pl.program_id(1)
    nr = tm // FFN_PIECES

    @pl.when(j == 0)
    def _():
        if real_rows_only:
            prev = prev_ref[...]
            xp = (_rms_rows(prev, prev.shape[-1]) * g2_ref[...]).astype(BF16)
            u_prev = jnp.dot(xp, wup_ref[...], preferred_element_type=F32)
            carry_ref[...] = u_prev[N_META - _HALO:N_META, :]
        else:
            carry_ref[...] = jnp.zeros_like(carry_ref)

    def up(s):
        rows = slice(s * nr, (s + 1) * nr)
        h = h_ref[rows, :]
        xn = _rms_rows(h, h.shape[-1]) * g2_ref[...]
        if not real_rows_only:
            row = j * tm + s * nr + lax.broadcasted_iota(jnp.int32, (nr, 1), 0)
            xn = jnp.where(row >= PAD, xn, 0.0)
        xn = xn.astype(BF16)
        for c in range(d_ff // _FF_CHUNK):
            acts = []
            for half in range(2):
                c0 = half * d_ff + c * _FF_CHUNK
                cols = slice(c0, c0 + _FF_CHUNK)
                u = jnp.dot(xn, wup_ref[:, cols], preferred_element_type=F32)
                ubuf_ref[half, 0:_HALO, :] = carry_ref[:, cols]
                ubuf_ref[half, _HALO:_HALO + nr, :] = u
                carry_ref[:, cols] = u[nr - _HALO:nr, :]
                u1 = ubuf_ref[half, _HALO - 1:_HALO - 1 + nr, :]
                u2 = ubuf_ref[half, _HALO - 2:_HALO - 2 + nr, :]
                acts.append(cb_ref[:, cols] + cw_ref[2:3, cols] * u + cw_ref[1:2, cols] * u1 + cw_ref[0:1, cols] * u2)
            gate, val = acts
            act_ref[rows, c * _FF_CHUNK:(c + 1) * _FF_CHUNK] = (gate * jax.nn.sigmoid(gate) * val).astype(BF16)

    def down(s):
        rows = slice(s * nr, (s + 1) * nr)
        out_ref[rows, :] = h_ref[rows, :] + jnp.dot(act_ref[rows, :], wdn_ref[...], preferred_element_type=F32)

    up(0)
    for s in range(FFN_PIECES):
        if s + 1 < FFN_PIECES:
            up(s + 1)
        down(s)


def _ffn(h, p, tm, real_rows_only=False):
    b, lp, d = h.shape
    row_blk = pl.BlockSpec((None, tm, d), lambda bi, j: (bi, j, 0))
    n_rows = lp - FRONT if real_rows_only else lp
    if real_rows_only:
        h_spec = pl.BlockSpec((None, pl.Element(tm), pl.Element(d)),
                              lambda bi, j: (bi, pl.multiple_of(FRONT + j * tm, LANES), 0))
        prev_spec = pl.BlockSpec((None, pl.Element(N_META), pl.Element(d)), lambda bi, j: (bi, PAD, 0))
    else:
        h_spec = row_blk
        prev_spec = pl.BlockSpec((None, N_META, d), lambda bi, j: (bi, 0, 0))
    whole = [_whole(p[k], pipeline_mode=pl.Buffered(1)) for k in ('g2', 'wup', 'cw', 'cb', 'wdn')]
    ws = [a for a, _ in whole]
    d_ff = ws[-1].shape[-2]
    return pl.pallas_call(
        functools.partial(_ffn_kernel, tm=tm, d_ff=d_ff, real_rows_only=real_rows_only),
        grid=(b, n_rows // tm),
        in_specs=[h_spec, prev_spec] + [spec for _, spec in whole],
        out_specs=row_blk,
        out_shape=jax.ShapeDtypeStruct((b, n_rows, d), F32),
        scratch_shapes=[pltpu.VMEM((_HALO, 2 * d_ff), F32),
                        pltpu.VMEM((2, _HALO + tm, _FF_CHUNK), F32),
                        pltpu.VMEM((tm, d_ff), BF16)],
        compiler_params=pltpu.CompilerParams(dimension_semantics=("arbitrary", "arbitrary"),
                                             vmem_limit_bytes=VMEM_LIMIT),
        name="conv_ffn",
    )(h, h, *ws)


def _constants(lp, tm):
    half = ROPE_DIM // 2
    freqs = ROPE_THETA ** (-np.arange(half, dtype=np.float64) / half)
    lane = np.arange(LANES)
    ang = (np.arange(lp) - PAD).astype(np.float64)[:, None] * freqs[lane % half][None, :]
    sign = np.where((lane % ROPE_DIM) < half, -1.0, 1.0)
    row = np.arange(lp)
    pos_hi, pos_lo = row // LANES, row % LANES
    swa_kaux = np.zeros((lp, LANES), np.float32)
    swa_qaux = np.zeros((lp, BRANCH_WIDTH), np.float32)
    swa_masks = np.zeros((2, 2 * LANES), np.float32)
    for a in range(2):
        swa_masks[a, a * HEAD_DIM:(a + 1) * HEAD_DIM] = 1.0
        swa_masks[a, LANES + 4 * a:LANES + 4 * (a + 1)] = 1.0
        swa_kaux[:, 4 * a:4 * (a + 1)] = np.stack([np.ones(lp), np.ones(lp), pos_hi, pos_lo], axis=1)
        for t in range(4):
            slope = 2.0 ** (-8.0 * (t + 4 * a + 1) / N_HEADS)
            swa_qaux[:, t * LANES + 4 * a:t * LANES + 4 * (a + 1)] = np.stack(
                [-LANES * slope * pos_hi, -slope * pos_lo, np.full(lp, LANES * slope), np.full(lp, slope)], axis=1)
    tri = np.tril(np.ones((tm, tm), np.float32))
    eq = np.zeros((DEC_PARTS, LANES, LANES), np.float32)
    ek = np.zeros((DEC_PARTS, LANES, LANES), np.float32)
    qconst = np.zeros((1, LANES), np.float32)
    kconst = np.zeros((1, LANES), np.float32)
    for hd in range(N_HEADS):
        base = 2 * DEC_PARTS * hd
        for i in range(DEC_PARTS):
            eq[i, hd, base + i] = 1.0
            ek[i, hd, base + DEC_PARTS + i] = -1.0
            qconst[0, base + DEC_PARTS + i] = 1.0
            kconst[0, base + i] = 1.0
    fox_masks = np.zeros((N_HEADS // 2, 2, 2 * LANES), np.float32)
    mla_masks = np.zeros((N_HEADS // 2, 2, 2 * LANES), np.float32)
    for p in range(N_HEADS // 2):
        for a in range(2):
            hd = 2 * p + a
            fox_masks[p, a, a * HEAD_DIM:(a + 1) * HEAD_DIM] = 1.0
            fox_masks[p, a, LANES + 2 * DEC_PARTS * hd:LANES + 2 * DEC_PARTS * (hd + 1)] = 1.0
            mla_masks[p, a, a * HEAD_DIM:(a + 1) * HEAD_DIM] = 1.0
            g = hd % 4
            mla_masks[p, a, LANES + g * ROPE_DIM:LANES + (g + 1) * ROPE_DIM] = 1.0
    return {
        'cos': jnp.asarray(np.cos(ang), F32), 'sin': jnp.asarray(np.sin(ang) * sign[None, :], F32),
        'tri': jnp.asarray(tri, BF16), 'eq': jnp.asarray(eq, BF16), 'ek': jnp.asarray(ek, BF16),
        'qconst': jnp.asarray(qconst), 'kconst': jnp.asarray(kconst),
        'fox_masks': jnp.asarray(fox_masks, BF16), 'mla_masks': jnp.asarray(mla_masks, BF16),
        'swa_masks': jnp.asarray(swa_masks, BF16), 'swa_qaux': jnp.asarray(swa_qaux, BF16),
        'swa_kaux': jnp.asarray(swa_kaux, BF16),
    }


def _swa_head_order(a, axis):
    shape = a.shape
    a = a.reshape(shape[:axis] + (SWA_KV_HEADS, N_HEADS // SWA_KV_HEADS, HEAD_DIM) + shape[axis + 1:])
    return jnp.swapaxes(a, axis, axis + 1).reshape(shape)


_O_FQ, _O_FK, _O_FV, _O_FF = 0, 512, 1024, 1536
_O_CQ = _O_FF + N_HEADS
_O_CKV = _O_CQ + MLA_Q_RANK
_O_KR = _O_CKV + MLA_KV_RANK
_O_SQ = _O_KR + ROPE_DIM
_O_SK = _O_SQ + 512
_O_SV = _O_SK + SWA_KV_HEADS * HEAD_DIM
_O_GATES = _O_SV + SWA_KV_HEADS * HEAD_DIM


def _stacked_weights(w_in, w_branch, w_o, ffn_w_up, ffn_w_down):
    n_layers, d, _ = w_in.shape
    w_in = w_in.astype(BF16)
    col = lambda c0, n: w_in[:, :, c0:c0 + n]
    wa = jnp.concatenate([
        col(_O_FQ, 512), col(_O_FK, 512), _swa_head_order(col(_O_SQ, 512), 2), col(_O_SK, 128),
        col(_O_CQ, MLA_Q_RANK), col(_O_CKV, MLA_KV_RANK),
        jnp.tile(col(_O_KR, ROPE_DIM), (1, 1, LANES // ROPE_DIM)),
        col(_O_FF, N_HEADS), jnp.zeros((n_layers, d, LANES - N_HEADS), w_in.dtype)], axis=2).astype(BF16)
    wb = jnp.concatenate([w_branch[:, :2], _swa_head_order(w_branch[:, 2:3], 2)], axis=1).astype(BF16)
    return {'wa': wa, 'wg': w_in[:, :, _O_GATES:].astype(BF16), 'wb': wb, 'wo': w_o.astype(BF16),
            'wup': ffn_w_up.astype(BF16), 'wdn': ffn_w_down.astype(BF16)}


def _layer_params(l, stacked, norm1_g, w_in, fox_forget_b, fox_q_g, fox_k_g, mla_q_a_g, mla_w_q_up, mla_kv_a_g,
                  mla_w_kv_up, mla_q_g, mla_k_g, swa_q_g, swa_k_g, swa_sinks, w_branch, w_o,
                  norm2_g, ffn_w_up, ffn_conv_w, ffn_conv_b, ffn_w_down):
    w = w_in[l]
    tile_row = lambda g, n: jnp.tile(g, n)[None, :].astype(F32)
    width = HEAD_DIM + ROPE_DIM
    qcols = np.concatenate([np.concatenate([np.arange(hd * width, hd * width + HEAD_DIM) for hd in range(N_HEADS)]),
                            np.concatenate([np.arange(hd * width + HEAD_DIM, (hd + 1) * width) for hd in range(N_HEADS)])])
    kcols = np.concatenate([np.arange(hd * 2 * HEAD_DIM, hd * 2 * HEAD_DIM + HEAD_DIM) for hd in range(N_HEADS)])
    vcols = kcols + HEAD_DIM
    fb = jnp.concatenate([fox_forget_b[l], jnp.zeros((LANES - N_HEADS,), F32)])[None, :]
    return {
        'g1': norm1_g[l][None, :], 'wa': (stacked['wa'], l),
        'gfq': tile_row(fox_q_g[l], N_HEADS) * (LOG2E * HEAD_DIM ** -0.5), 'gfk': tile_row(fox_k_g[l], N_HEADS),
        'gsq': tile_row(swa_q_g[l], N_HEADS) * (HEAD_DIM ** -0.5), 'gsk': tile_row(swa_k_g[l], SWA_KV_HEADS),
        'fb': fb,
        'gcq': mla_q_a_g[l][None, :], 'wqu': mla_w_q_up[l][:, qcols].astype(BF16),
        'gckv': mla_kv_a_g[l][None, :], 'wkvu': mla_w_kv_up[l][:, kcols].astype(BF16),
        'wvtf': w[:, _O_FV:_O_FV + 512].T.astype(BF16), 'wvtm': mla_w_kv_up[l][:, vcols].T.astype(BF16),
        'wvts': w[:, _O_SV:_O_SV + SWA_KV_HEADS * HEAD_DIM].T.astype(BF16),
        'gmq_n': tile_row(mla_q_g[l][:HEAD_DIM], N_HEADS) * (LOG2E * width ** -0.5),
        'gmq_r': tile_row(mla_q_g[l][HEAD_DIM:], LANES // ROPE_DIM) * (LOG2E * width ** -0.5),
        'gmk_n': tile_row(mla_k_g[l][:HEAD_DIM], N_HEADS),
        'gmk_r': tile_row(mla_k_g[l][HEAD_DIM:], LANES // ROPE_DIM),
        'sinks': swa_sinks[l].astype(F32),
        'wg': (stacked['wg'], l), 'wb': (stacked['wb'], l), 'wo': (stacked['wo'], l),
        'g2': norm2_g[l][None, :], 'wup': (stacked['wup'], l), 'cw': ffn_conv_w[l], 'cb': ffn_conv_b[l][None, :],
        'wdn': (stacked['wdn'], l),
    }


def kernel(x, meta_tokens, norm1_g, w_in, fox_forget_b, fox_q_g, fox_k_g, mla_q_a_g, mla_w_q_up, mla_kv_a_g, mla_w_kv_up, mla_q_g, mla_k_g, swa_q_g, swa_k_g, swa_sinks, w_branch, w_o, norm2_g, ffn_w_up, ffn_conv_w, ffn_conv_b, ffn_w_down):
    b, seq, d = x.shape
    assert seq % LANES == 0 and meta_tokens.shape == (N_META, d)
    lp = FRONT + seq
    tm = _row_tile(lp)
    tq = CAUSAL_BLOCK if seq % (2 * CAUSAL_BLOCK) == 0 else _row_tile(seq)
    consts = _constants(lp, IN_SUB)
    front = jnp.concatenate([jnp.zeros((PAD, d), x.dtype), meta_tokens.astype(x.dtype)], axis=0)
    h = x
    weights = (norm1_g, w_in, fox_forget_b, fox_q_g, fox_k_g, mla_q_a_g, mla_w_q_up, mla_kv_a_g, mla_w_kv_up,
               mla_q_g, mla_k_g, swa_q_g, swa_k_g, swa_sinks, w_branch, w_o, norm2_g, ffn_w_up, ffn_conv_w,
               ffn_conv_b, ffn_w_down)
    stacked = _stacked_weights(w_in, w_branch, w_o, ffn_w_up, ffn_w_down)
    for l in range(w_in.shape[0]):
        p = _layer_params(l, stacked, *weights)
        qf, kf, vtf, qm, km, vtm, sq, sk, vts = _in_proj(h, front, p, consts, tm, from_x=l == 0)
        out_a = _causal_attn(qf, kf, vtf, consts['fox_masks'], tq)
        out_b = _causal_attn(qm, km, vtm, consts['mla_masks'], tq)
        out_c = _swa_attn(sq, sk, vts, p['sinks'], consts)
        h = _merge(h, front, out_a, out_b, out_c, p, tm, from_x=l == 0)
        last = l == w_in.shape[0] - 1
        h = _ffn(h, p, _row_tile(seq) if last else tm, real_rows_only=last)
    return h
```
